```python
import jax
import jax.numpy as jnp
from jax import lax
import numpy as np

D_MODEL = 4096
BATCH = 1
SEQ = 8192
DEPTH = 4

HEAD_DIM = 128
N_HEADS = D_MODEL // 512
W_MIX = N_HEADS * HEAD_DIM
N_BRANCH = 3
N_IN_GROUPS = 10
D_FF = D_MODEL
CONV_K = 4
Q_BLOCK = 128
CHUNK = 64
EPS = 1e-6
NEG_BIG = -1e30
LB_FLOOR = 1e-30

kernel_name = 'hybrid_stickbreak_hgrn2_mlstm_macaron'


def rmsnorm(x, gain, eps=EPS):
    xf = x.astype(jnp.float32)
    y = xf * lax.rsqrt(jnp.mean(xf * xf, axis=-1, keepdims=True) + eps)
    return (y * gain.astype(jnp.float32)).astype(x.dtype)


def swiglu(h, w_gate, w_up, w_down):
    return (jax.nn.silu(h @ w_gate) * (h @ w_up)) @ w_down


def split_heads(a):
    b, s, _ = a.shape
    return a.reshape(b, s, N_HEADS, HEAD_DIM).transpose(0, 2, 1, 3)


def merge_heads(a):
    b, h, s, d = a.shape
    return a.transpose(0, 2, 1, 3).reshape(b, s, h * d)


def to_chunks(a):
    b, h, s = a.shape[:3]
    a = a.reshape((b, h, s // CHUNK, CHUNK) + a.shape[3:])
    return jnp.moveaxis(a, 2, 0)


def from_chunks(a):
    a = jnp.moveaxis(a, 0, 2)
    b, h, n, l = a.shape[:4]
    return a.reshape((b, h, n * l) + a.shape[4:])


def stick_breaking_attention(q, k, v):
    b, h, s, d = q.shape
    n_blk = s // Q_BLOCK
    scale = d ** -0.5
    q_blocks = jnp.moveaxis(q.reshape(b, h, n_blk, Q_BLOCK, d), 2, 0)
    key_pos = jnp.arange(s)

    def one_block(args):
        blk, qb = args
        z = jnp.einsum('bhtd,bhsd->bhts', qb, k) * scale
        q_pos = blk * Q_BLOCK + jnp.arange(Q_BLOCK)
        mask = key_pos[None, :] < q_pos[:, None]
        log_1m = jnp.where(mask, jax.nn.log_sigmoid(-z), 0.0)
        between = lax.cumsum(log_1m, axis=3, reverse=True) - log_1m
        log_att = jnp.where(mask, jax.nn.log_sigmoid(z) + between, NEG_BIG)
        att = jnp.exp(log_att)
        return jnp.einsum('bhts,bhsd->bhtd', att, v)

    out = lax.map(one_block, (jnp.arange(n_blk), q_blocks))
    return jnp.moveaxis(out, 0, 2).reshape(b, h, s, d)


def hgrn2_recurrence(q, k, v, log_f):
    b, h, s, dk = q.shape
    mask = jnp.tril(jnp.ones((CHUNK, CHUNK), dtype=bool))

    def step(state, inp):
        qc, kc, vc, lf = inp
        cum = jnp.cumsum(lf, axis=2)
        diff = jnp.where(mask[:, :, None],
                         cum[:, :, :, None, :] - cum[:, :, None, :, :], NEG_BIG)
        att = jnp.einsum('bhtd,bhtsd,bhsd->bhts', qc, jnp.exp(diff), kc)
        last = cum[:, :, -1, :]
        out = att @ vc + jnp.einsum('bhtd,bhdv->bhtv', qc * jnp.exp(cum), state)
        new_state = (jnp.exp(last)[..., None] * state
                     + jnp.einsum('bhsd,bhsv->bhdv', kc * jnp.exp(last[:, :, None, :] - cum), vc))
        return new_state, out

    init = jnp.zeros((b, h, dk, v.shape[-1]), jnp.float32)
    _, out = lax.scan(step, init, (to_chunks(q), to_chunks(k), to_chunks(v), to_chunks(log_f)))
    return from_chunks(out)


def mlstm_recurrence(q, k, v, log_i, log_f):
    b, h, s, d = q.shape
    mask = jnp.tril(jnp.ones((CHUNK, CHUNK), dtype=bool))

    def step(carry, inp):
        c_mem, n_mem, m = carry
        qc, kc, vc, li, lf = inp
        cum = jnp.cumsum(lf, axis=-1)
        dmat = jnp.where(mask, cum[..., :, None] - cum[..., None, :] + li[..., None, :], NEG_BIG)
        inter = cum + m[..., None]
        m_t = jnp.maximum(inter, jnp.max(dmat, axis=-1))
        w = jnp.exp(dmat - m_t[..., None]) * jnp.einsum('bhtd,bhsd->bhts', qc, kc)
        carry_w = jnp.exp(inter - m_t)
        num = w @ vc + carry_w[..., None] * jnp.einsum('bhtd,bhdv->bhtv', qc, c_mem)
        den = jnp.sum(w, axis=-1) + carry_w * jnp.einsum('bhtd,bhd->bht', qc, n_mem)
        h_out = num / jnp.maximum(jnp.abs(den), jnp.exp(-m_t))[..., None]
        last = cum[..., -1]
        g = last[..., None] - cum + li
        m_new = jnp.maximum(last + m, jnp.max(g, axis=-1))
        wk = jnp.exp(g - m_new[..., None])
        decay = jnp.exp(last + m - m_new)
        c_new = decay[..., None, None] * c_mem + jnp.einsum('bhs,bhsd,bhsv->bhdv', wk, kc, vc)
        n_new = decay[..., None] * n_mem + jnp.einsum('bhs,bhsd->bhd', wk, kc)
        return (c_new, n_new, m_new), h_out

    init = (jnp.zeros((b, h, d, v.shape[-1]), jnp.float32),
            jnp.zeros((b, h, d), jnp.float32),
            jnp.zeros((b, h), jnp.float32))
    _, out = lax.scan(step, init, (to_chunks(q), to_chunks(k), to_chunks(v),
                                   to_chunks(log_i), to_chunks(log_f)))
    return from_chunks(out)


def causal_short_conv(x, w, bias):
    y = lax.conv_general_dilated(x, w[:, None, :].astype(x.dtype), window_strides=(1,),
                                 padding=[(CONV_K - 1, 0)],
                                 dimension_numbers=('NWC', 'WIO', 'NWC'),
                                 feature_group_count=x.shape[-1])
    return y + bias


def hybrid_mixer(h, lower_bound, w_in, sb_q_gain, sb_k_gain, hg_out_gain, ml_conv_w, ml_conv_b,
                 ml_w_q, ml_w_k, ml_w_if, ml_b_if, ml_out_gain, ml_skip, w_merge_gate,
                 w_branch_a, w_branch_b, w_branch_c, w_out):
    f32 = jnp.float32
    b, s, _ = h.shape
    proj = h @ w_in
    (sb_q, sb_k, sb_v, hg_f, hg_i, hg_q, hg_g, ml_x, ml_v, ml_z) = jnp.split(proj, N_IN_GROUPS, axis=-1)

    q_a = rmsnorm(split_heads(sb_q).astype(f32), sb_q_gain)
    k_a = rmsnorm(split_heads(sb_k).astype(f32), sb_k_gain)
    y_a = merge_heads(stick_breaking_attention(q_a, k_a, split_heads(sb_v).astype(f32))).astype(h.dtype)

    lb = lower_bound.astype(f32).reshape(N_HEADS, 1, HEAD_DIM)
    f_pre = split_heads(hg_f).astype(f32)
    log_f_b = jnp.logaddexp(jnp.log(jnp.maximum(lb, LB_FLOOR)),
                            jnp.log1p(-lb) + jax.nn.log_sigmoid(f_pre))
    k_b = (1.0 - lb) * jax.nn.sigmoid(-f_pre)
    o_b = hgrn2_recurrence(jax.nn.silu(split_heads(hg_q).astype(f32)), k_b,
                           split_heads(hg_i).astype(f32), log_f_b)
    y_b = (merge_heads(rmsnorm(o_b, hg_out_gain)) * jax.nn.silu(hg_g.astype(f32))).astype(h.dtype)

    xc_act = jax.nn.silu(causal_short_conv(ml_x, ml_conv_w, ml_conv_b))
    xh = xc_act.reshape(b, s, N_HEADS, HEAD_DIM)
    q_c = jnp.einsum('bshd,hde->bshe', xh, ml_w_q)
    k_c = jnp.einsum('bshd,hde->bshe', xh, ml_w_k)
    gates = jnp.concatenate([q_c.reshape(b, s, W_MIX), k_c.reshape(b, s, W_MIX), ml_v], axis=-1) @ ml_w_if + ml_b_if
    gates = gates.astype(f32).transpose(0, 2, 1)
    log_i_c = gates[:, :N_HEADS]
    log_f_c = jax.nn.log_sigmoid(gates[:, N_HEADS:])
    h_c = mlstm_recurrence(q_c.transpose(0, 2, 1, 3).astype(f32),
                           k_c.transpose(0, 2, 1, 3).astype(f32) * HEAD_DIM ** -0.5,
                           split_heads(ml_v).astype(f32), log_i_c, log_f_c)
    y_c = ((merge_heads(rmsnorm(h_c, ml_out_gain)) + ml_skip * xc_act) * jax.nn.silu(ml_z)).astype(h.dtype)

    branches = (y_a @ w_branch_a, y_b @ w_branch_b, y_c @ w_branch_c)
    merged = sum(jax.nn.sigmoid(h @ w_merge_gate[i]) * branches[i] for i in range(N_BRANCH))
    return merged @ w_out


def setup_inputs(seed: int = 0) -> dict:
    key = jax.random.key(seed)
    ks = jax.random.split(key, 32)
    L, D, F, W, H = DEPTH, D_MODEL, D_FF, W_MIX, N_HEADS

    def nrm(i, shape, scale):
        return jax.random.normal(ks[i], shape, jnp.float32) * scale

    def gain(i, shape):
        return 1.0 + nrm(i, shape, 0.02)

    ml_b_if = jnp.concatenate([nrm(16, (L, H), 0.1),
                               jnp.linspace(3.0, 6.0, H, dtype=jnp.float32)[None, :] + nrm(17, (L, H), 0.1)], axis=1)
    return {
        'x': nrm(0, (BATCH, SEQ, D), 1.0),
        'ffn1_norm': gain(1, (L, D)),
        'ffn1_w_gate': nrm(2, (L, D, F), D ** -0.5),
        'ffn1_w_up': nrm(3, (L, D, F), D ** -0.5),
        'ffn1_w_down': nrm(4, (L, F, D), F ** -0.5),
        'mix_norm': gain(5, (L, D)),
        'w_in': nrm(6, (L, D, N_IN_GROUPS * W), D ** -0.5),
        'sb_q_gain': gain(7, (L, HEAD_DIM)),
        'sb_k_gain': gain(8, (L, HEAD_DIM)),
        'hg_lb_logits': nrm(9, (L, W), 0.5),
        'hg_out_gain': gain(10, (L, HEAD_DIM)),
        'ml_conv_w': nrm(11, (L, CONV_K, W), CONV_K ** -0.5),
        'ml_conv_b': nrm(12, (L, W), 0.01),
        'ml_w_q': nrm(13, (L, H, HEAD_DIM, HEAD_DIM), HEAD_DIM ** -0.5),
        'ml_w_k': nrm(14, (L, H, HEAD_DIM, HEAD_DIM), HEAD_DIM ** -0.5),
        'ml_w_if': nrm(15, (L, 3 * W, 2 * H), (3 * W) ** -0.5),
        'ml_b_if': ml_b_if,
        'ml_out_gain': gain(18, (L, HEAD_DIM)),
        'ml_skip': gain(19, (L, W)),
        'w_merge_gate': nrm(20, (L, N_BRANCH, D, D), D ** -0.5),
        'w_branch_a': nrm(21, (L, W, D), W ** -0.5),
        'w_branch_b': nrm(22, (L, W, D), W ** -0.5),
        'w_branch_c': nrm(23, (L, W, D), W ** -0.5),
        'w_out': nrm(24, (L, D, D), D ** -0.5),
        'ffn2_norm': gain(25, (L, D)),
        'ffn2_w_gate': nrm(26, (L, D, F), D ** -0.5),
        'ffn2_w_up': nrm(27, (L, D, F), D ** -0.5),
        'ffn2_w_down': nrm(28, (L, F, D), F ** -0.5),
    }


def reference(x, ffn1_norm, ffn1_w_gate, ffn1_w_up, ffn1_w_down, mix_norm, w_in, sb_q_gain, sb_k_gain,
              hg_lb_logits, hg_out_gain, ml_conv_w, ml_conv_b, ml_w_q, ml_w_k, ml_w_if, ml_b_if,
              ml_out_gain, ml_skip, w_merge_gate, w_branch_a, w_branch_b, w_branch_c, w_out,
              ffn2_norm, ffn2_w_gate, ffn2_w_up, ffn2_w_down):
    p = jax.nn.softmax(hg_lb_logits.astype(jnp.float32), axis=0)
    lower_bounds = jnp.cumsum(p, axis=0) - p[0]
    for l in range(DEPTH):
        x = x + 0.5 * swiglu(rmsnorm(x, ffn1_norm[l]), ffn1_w_gate[l], ffn1_w_up[l], ffn1_w_down[l])
        x = x + hybrid_mixer(rmsnorm(x, mix_norm[l]), lower_bounds[l], w_in[l], sb_q_gain[l], sb_k_gain[l],
                             hg_out_gain[l], ml_conv_w[l], ml_conv_b[l], ml_w_q[l], ml_w_k[l], ml_w_if[l],
                             ml_b_if[l], ml_out_gain[l], ml_skip[l], w_merge_gate[l], w_branch_a[l],
                             w_branch_b[l], w_branch_c[l], w_out[l])
        x = x + 0.5 * swiglu(rmsnorm(x, ffn2_norm[l]), ffn2_w_gate[l], ffn2_w_up[l], ffn2_w_down[l])
    return x
```

```python
import functools

import jax
import jax.numpy as jnp
from jax import lax
from jax.experimental import pallas as pl
from jax.experimental.pallas import tpu as pltpu

F32 = jnp.float32
BF16 = jnp.bfloat16

HEAD_DIM = 128
EPS = 1e-6
NEG_BIG = -1e30
LB_FLOOR = 1e-30

V7X_VMEM_LIMIT_BYTES = 56 * 1024 * 1024
ROW_TILE = 1024
COL_TILE = 512
ATT_TILE = 256
REC_CHUNK = 128
REC_ROWS = 512
SUB = 16


def _params(*sem):
    return pltpu.CompilerParams(dimension_semantics=sem, vmem_limit_bytes=V7X_VMEM_LIMIT_BYTES)


def _dot(a, b):
    return jnp.dot(a, b, preferred_element_type=F32)


def _dot_nt(a, b):
    return lax.dot_general(a, b, (((1,), (1,)), ((), ())), preferred_element_type=F32)


def _split3(x):
    hi = x.astype(BF16)
    r1 = x - hi.astype(F32)
    mid = r1.astype(BF16)
    lo = (r1 - mid.astype(F32)).astype(BF16)
    return hi, mid, lo


def _softplus(z):
    return jnp.maximum(z, 0.0) + jnp.log1p(jnp.exp(-jnp.abs(z)))


def _silu(x):
    return x * jax.nn.sigmoid(x)


def _rmsnorm_kernel(x_ref, g_ref, o_ref):
    x = x_ref[...]
    ms = jnp.mean(x * x, axis=-1, keepdims=True)
    o_ref[...] = (x * lax.rsqrt(ms + EPS) * g_ref[...]).astype(o_ref.dtype)


def _rmsnorm_cast(x, gain):
    s, d = x.shape
    bm = min(256, s)
    return pl.pallas_call(
        _rmsnorm_kernel,
        grid=(s // bm,),
        in_specs=[pl.BlockSpec((bm, d), lambda i: (i, 0)),
                  pl.BlockSpec((1, d), lambda i: (0, 0))],
        out_specs=pl.BlockSpec((bm, d), lambda i: (i, 0)),
        out_shape=jax.ShapeDtypeStruct((s, d), BF16),
        compiler_params=_params("parallel"),
        name="rmsnorm_cast",
    )(x, gain.reshape(1, d))


def _ffn_up_kernel(n_ref, wg_ref, wu_ref, o_ref):
    n = n_ref[...]
    g = _dot(n, wg_ref[...])
    u = _dot(n, wu_ref[...])
    o_ref[...] = (_silu(g) * u).astype(o_ref.dtype)


def _ffn_up(n, wg, wu):
    s, d = n.shape
    f = wg.shape[1]
    bm, bn = min(ROW_TILE, s), min(COL_TILE, f)
    return pl.pallas_call(
        _ffn_up_kernel,
        grid=(s // bm, f // bn),
        in_specs=[pl.BlockSpec((bm, d), lambda i, j: (i, 0)),
                  pl.BlockSpec((d, bn), lambda i, j: (0, j)),
                  pl.BlockSpec((d, bn), lambda i, j: (0, j))],
        out_specs=pl.BlockSpec((bm, bn), lambda i, j: (i, j)),
        out_shape=jax.ShapeDtypeStruct((s, f), BF16),
        compiler_params=_params("parallel", "arbitrary"),
        name="ffn_up",
    )(n, wg, wu)


def _mm_res_kernel(a_ref, w_ref, x_ref, o_ref, *, scale):
    o_ref[...] = x_ref[...] + scale * _dot(a_ref[...], w_ref[...])


def _mm_residual(a, w, x, scale):
    s, k = a.shape
    n = w.shape[1]
    bm, bn = min(ROW_TILE, s), min(COL_TILE, n)
    return pl.pallas_call(
        functools.partial(_mm_res_kernel, scale=scale),
        grid=(s // bm, n // bn),
        in_specs=[pl.BlockSpec((bm, k), lambda i, j: (i, 0)),
                  pl.BlockSpec((k, bn), lambda i, j: (0, j)),
                  pl.BlockSpec((bm, bn), lambda i, j: (i, j))],
        out_specs=pl.BlockSpec((bm, bn), lambda i, j: (i, j)),
        out_shape=jax.ShapeDtypeStruct((s, n), F32),
        compiler_params=_params("parallel", "arbitrary"),
        name="mm_residual",
    )(a, w, x)


def _mm_kernel(a_ref, w_ref, o_ref):
    o_ref[...] = _dot(a_ref[...], w_ref[...]).astype(o_ref.dtype)


def _mm(a, w, n_out, bn, col_block_of, out_dtype):
    s, k = a.shape
    bm = min(ROW_TILE, s)
    return pl.pallas_call(
        _mm_kernel,
        grid=(s // bm, n_out // bn),
        in_specs=[pl.BlockSpec((bm, k), lambda i, j: (i, 0)),
                  pl.BlockSpec((k, bn), lambda i, j: (0, col_block_of(j)))],
        out_specs=pl.BlockSpec((bm, bn), lambda i, j: (i, j)),
        out_shape=jax.ShapeDtypeStruct((s, n_out), out_dtype),
        compiler_params=_params("parallel", "arbitrary"),
        name="in_proj",
    )(a, w)


def _in_proj(hn, w_in, w_mix):
    bn = min(COL_TILE, w_mix)
    nblk = w_mix // bn
    proj = _mm(hn, w_in, 9 * w_mix, bn, lambda j: j + jnp.where(j >= 3 * nblk, nblk, 0), BF16)
    f_pre = _mm(hn, w_in, w_mix, bn, lambda j: j + 3 * nblk, F32)
    return proj, f_pre


def _qk_norm_kernel(x_ref, g_ref, o_ref, *, n_heads):
    g = g_ref[0]
    for h in range(n_heads):
        sl = slice(h * HEAD_DIM, (h + 1) * HEAD_DIM)
        x = x_ref[:, sl].astype(F32)
        ms = jnp.mean(x * x, axis=-1, keepdims=True)
        o_ref[:, sl] = (x * lax.rsqrt(ms + EPS) * g).astype(o_ref.dtype)


def _qk_norm(proj, gains, w_mix):
    s = proj.shape[0]
    bm = min(512, s)
    return pl.pallas_call(
        functools.partial(_qk_norm_kernel, n_heads=w_mix // HEAD_DIM),
        grid=(s // bm, 2),
        in_specs=[pl.BlockSpec((bm, w_mix), lambda i, g: (i, g)),
                  pl.BlockSpec((1, 1, HEAD_DIM), lambda i, g: (g, 0, 0))],
        out_specs=pl.BlockSpec((bm, w_mix), lambda i, g: (i, g)),
        out_shape=jax.ShapeDtypeStruct((s, 2 * w_mix), BF16),
        compiler_params=_params("parallel", "arbitrary"),
        name="qk_norm",
    )(proj, gains)


def _sb_kernel(q_ref, k_ref, v_ref, o_ref, *, tile):
    qi = pl.program_id(1)
    q = q_ref[...]
    row = lax.broadcasted_iota(jnp.int32, (tile, tile), 0)
    col = lax.broadcasted_iota(jnp.int32, (tile, tile), 1)
    later = (row > col).astype(BF16)
    causal = col < row

    def block(kb, acc, run, masked):
        off = pl.multiple_of(kb * tile, tile)
        kblk = k_ref[pl.ds(off, tile), :]
        vblk = v_ref[pl.ds(off, tile), :]
        z = _dot_nt(q, kblk)
        sp = _softplus(z)
        log_1m = -sp
        if masked:
            log_1m = jnp.where(causal, log_1m, 0.0)
        hi = log_1m.astype(BF16)
        lo = (log_1m - hi.astype(F32)).astype(BF16)
        between = _dot(hi, later) + _dot(lo, later)
        log_att = (z - sp) + between + run
        att = jnp.exp(log_att)
        if masked:
            att = jnp.where(causal, att, 0.0)
        acc = acc + _dot(att.astype(BF16), vblk)
        run = run + jnp.sum(log_1m, axis=-1, keepdims=True)
        return acc, run

    acc0 = jnp.zeros((tile, HEAD_DIM), F32)
    run0 = jnp.zeros((tile, 1), F32)
    acc, run = block(qi, acc0, run0, True)

    def body(it, carry):
        return block(qi - 1 - it, carry[0], carry[1], False)

    acc, run = lax.fori_loop(0, qi, body, (acc, run))
    o_ref[...] = acc.astype(o_ref.dtype)


def _sb_attention(qk, proj, w_mix):
    s = qk.shape[0]
    h = w_mix // HEAD_DIM
    tile = min(ATT_TILE, s)
    return pl.pallas_call(
        functools.partial(_sb_kernel, tile=tile),
        grid=(h, s // tile),
        in_specs=[pl.BlockSpec((tile, HEAD_DIM), lambda hh, i: (i, hh)),
                  pl.BlockSpec((s, HEAD_DIM), lambda hh, i: (0, h + hh)),
                  pl.BlockSpec((s, HEAD_DIM), lambda hh, i: (0, 2 * h + hh))],
        out_specs=pl.BlockSpec((tile, HEAD_DIM), lambda hh, i: (i, hh)),
        out_shape=jax.ShapeDtypeStruct((s, w_mix), BF16),
        compiler_params=_params("parallel", "arbitrary"),
        name="sb_attention",
    )(qk, qk, proj)


def _hgrn2_kernel(f_ref, v_ref, q_ref, g_ref, lbl_ref, gain_ref, o_ref, st_ref, *, layer, rows):
    @pl.when(pl.program_id(1) == 0)
    def _():
        st_ref[...] = jnp.zeros_like(st_ref)

    logits = lbl_ref[...]
    e = jnp.exp(logits - jnp.max(logits, axis=0, keepdims=True))
    p = e / jnp.sum(e, axis=0, keepdims=True)
    lb = jnp.zeros((1, HEAD_DIM), F32)
    for i in range(1, layer + 1):
        lb = lb + p[i:i + 1, :]
    log_lb = jnp.log(jnp.maximum(lb, LB_FLOOR))
    log_1m_lb = jnp.log1p(-lb)

    L = REC_CHUNK
    r_i = lax.broadcasted_iota(jnp.int32, (L, L), 0)
    c_i = lax.broadcasted_iota(jnp.int32, (L, L), 1)
    tri_incl = (c_i <= r_i).astype(BF16)
    sub_row = lax.broadcasted_iota(jnp.int32, (SUB, HEAD_DIM), 0)
    row_l = lax.broadcasted_iota(jnp.int32, (L, HEAD_DIM), 0)

    def chunk(c, carry):
        rs = pl.ds(pl.multiple_of(c * L, L), L)
        x = f_ref[rs, :]
        ls = jnp.minimum(x, 0.0) - jnp.log1p(jnp.exp(-jnp.abs(x)))
        b = log_1m_lb + ls
        lf = jnp.maximum(log_lb, b) + jnp.log1p(jnp.exp(-jnp.abs(log_lb - b)))
        kk = (1.0 - lb) * jax.nn.sigmoid(-x)
        qq = _silu(q_ref[rs, :].astype(F32))
        v = v_ref[rs, :]
        vf = v.astype(F32)

        hi, mid, lo = _split3(lf)
        cum = _dot(tri_incl, hi) + _dot(tri_incl, mid) + _dot(tri_incl, lo)
        last = cum[L - 1:L, :]

        st = st_ref[...]
        out = _dot_nt((qq * jnp.exp(cum)).astype(BF16), st.astype(BF16))

        a_rows = [jnp.zeros((SUB, L), F32)]
        for i in range(1, L // SUB):
            anchor = cum[i * SUB - 1:i * SUB, :]
            qt = qq[i * SUB:(i + 1) * SUB, :] * jnp.exp(cum[i * SUB:(i + 1) * SUB, :] - anchor)
            kt = jnp.where(row_l < i * SUB, kk * jnp.exp(jnp.minimum(anchor - cum, 0.0)), 0.0)
            a_rows.append(_dot_nt(qt.astype(BF16), kt.astype(BF16)))
        a_off = jnp.concatenate(a_rows, axis=0)
        out = out + _dot(a_off.astype(BF16), v)

        diag = []
        for i in range(L // SUB):
            bs = slice(i * SUB, (i + 1) * SUB)
            cb, qb, kb, vb = cum[bs, :], qq[bs, :], kk[bs, :], vf[bs, :]
            o_blk = jnp.zeros((SUB, HEAD_DIM), F32)
            for s_ in range(SUB):
                dec = jnp.exp(jnp.where(sub_row >= s_, cb - cb[s_:s_ + 1, :], NEG_BIG))
                a_col = jnp.sum(qb * dec * kb[s_:s_ + 1, :], axis=-1, keepdims=True)
                o_blk = o_blk + a_col * vb[s_:s_ + 1, :]
            diag.append(o_blk)
        out = out + jnp.concatenate(diag, axis=0)

        kt = kk * jnp.exp(last - cum)
        st_ref[...] = jnp.exp(last) * st + _dot(vf.T.astype(BF16), kt.astype(BF16))

        ms = jnp.mean(out * out, axis=-1, keepdims=True)
        y = out * lax.rsqrt(ms + EPS) * gain_ref[...]
        o_ref[rs, :] = (y * _silu(g_ref[rs, :].astype(F32))).astype(o_ref.dtype)
        return carry

    lax.fori_loop(0, rows // L, chunk, 0)


def _hgrn2(f_pre, proj, lb_logits, out_gain, w_mix, layer):
    s = f_pre.shape[0]
    h = w_mix // HEAD_DIM
    rows = min(REC_ROWS, s)
    depth = lb_logits.shape[0]
    blk = lambda g: pl.BlockSpec((rows, HEAD_DIM), lambda hh, c: (c, g * h + hh))
    return pl.pallas_call(
        functools.partial(_hgrn2_kernel, layer=layer, rows=rows),
        grid=(h, s // rows),
        in_specs=[pl.BlockSpec((rows, HEAD_DIM), lambda hh, c: (c, hh)),
                  blk(3), blk(4), blk(5),
                  pl.BlockSpec((depth, HEAD_DIM), lambda hh, c: (0, hh)),
                  pl.BlockSpec((1, HEAD_DIM), lambda hh, c: (0, 0))],
        out_specs=pl.BlockSpec((rows, HEAD_DIM), lambda hh, c: (c, hh)),
        out_shape=jax.ShapeDtypeStruct((s, w_mix), BF16),
        scratch_shapes=[pltpu.VMEM((HEAD_DIM, HEAD_DIM), F32)],
        compiler_params=_params("parallel", "arbitrary"),
        name="hgrn2",
    )(f_pre, proj, proj, proj, lb_logits, out_gain.reshape(1, HEAD_DIM))


def _mlstm_pre_kernel(x_ref, prev_ref, v_ref, cw_ref, cb_ref, wq_ref, wk_ref, wif_ref, bif_ref,
                      xc_ref, q_ref, k_ref, gates_ref, *, n_heads, conv_k):
    i = pl.program_id(0)
    x = x_ref[...].astype(F32)
    prev = prev_ref[8:16, :].astype(F32)
    prev = jnp.where(i > 0, prev, 0.0)
    bm = x.shape[0]
    row8 = lax.broadcasted_iota(jnp.int32, prev.shape, 0)
    acc = x * cw_ref[conv_k - 1:conv_k, :] + cb_ref[...]
    for back in range(1, conv_k):
        sh = pltpu.roll(x, back, 0)
        head = jnp.where(row8 < back, pltpu.roll(prev, back, 0), sh[:8, :])
        sh = jnp.concatenate([head, sh[8:, :]], axis=0)
        acc = acc + sh * cw_ref[conv_k - 1 - back:conv_k - back, :]
    xc = _silu(acc)
    xc_ref[...] = xc.astype(xc_ref.dtype)
    xcb = xc.astype(BF16)
    scale = HEAD_DIM ** -0.5
    w_mix = n_heads * HEAD_DIM
    g = jnp.zeros(gates_ref.shape, F32) + bif_ref[...]
    for h in range(n_heads):
        sl = slice(h * HEAD_DIM, (h + 1) * HEAD_DIM)
        qh = _dot(xcb[:, sl], wq_ref[h])
        kh = _dot(xcb[:, sl], wk_ref[h])
        q_ref[:, sl] = qh.astype(q_ref.dtype)
        k_ref[:, sl] = (kh * scale).astype(k_ref.dtype)
        g = g + _dot_nt(wif_ref[:, sl], qh.astype(BF16))
        g = g + _dot_nt(wif_ref[:, w_mix + h * HEAD_DIM:w_mix + (h + 1) * HEAD_DIM], kh.astype(BF16))
    g = g + _dot_nt(wif_ref[:, 2 * w_mix:], v_ref[...])
    gates_ref[...] = g


def _mlstm_pre(proj, conv_w, conv_b, w_q, w_k, w_if_t, b_if, w_mix):
    s = proj.shape[0]
    h = w_mix // HEAD_DIM
    bm = min(512, s)
    conv_k = conv_w.shape[0]
    return pl.pallas_call(
        functools.partial(_mlstm_pre_kernel, n_heads=h, conv_k=conv_k),
        grid=(s // bm,),
        in_specs=[pl.BlockSpec((bm, w_mix), lambda i: (i, 6)),
                  pl.BlockSpec((16, w_mix), lambda i: (jnp.maximum(i * (bm // 16) - 1, 0), 6)),
                  pl.BlockSpec((bm, w_mix), lambda i: (i, 7)),
                  pl.BlockSpec((conv_k, w_mix), lambda i: (0, 0)),
                  pl.BlockSpec((1, w_mix), lambda i: (0, 0)),
                  pl.BlockSpec((h, HEAD_DIM, HEAD_DIM), lambda i: (0, 0, 0)),
                  pl.BlockSpec((h, HEAD_DIM, HEAD_DIM), lambda i: (0, 0, 0)),
                  pl.BlockSpec((2 * h, 3 * w_mix), lambda i: (0, 0)),
                  pl.BlockSpec((2 * h, 1), lambda i: (0, 0))],
        out_specs=[pl.BlockSpec((bm, w_mix), lambda i: (i, 0)),
                   pl.BlockSpec((bm, w_mix), lambda i: (i, 0)),
                   pl.BlockSpec((bm, w_mix), lambda i: (i, 0)),
                   pl.BlockSpec((2 * h, bm), lambda i: (0, i))],
        out_shape=[jax.ShapeDtypeStruct((s, w_mix), BF16),
                   jax.ShapeDtypeStruct((s, w_mix), BF16),
                   jax.ShapeDtypeStruct((s, w_mix), BF16),
                   jax.ShapeDtypeStruct((2 * h, s), F32)],
        compiler_params=_params("parallel"),
        name="mlstm_pre",
    )(proj, proj, proj, conv_w, conv_b.reshape(1, w_mix), w_q, w_k, w_if_t, b_if.reshape(2 * h, 1))


def _mlstm_kernel(q_ref, k_ref, v_ref, li_ref, gf_ref, xc_ref, z_ref, skip_ref, gain_ref,
                  o_ref, c_ref, n_ref, m_ref, *, rows):
    @pl.when(pl.program_id(1) == 0)
    def _():
        c_ref[...] = jnp.zeros_like(c_ref)
        n_ref[...] = jnp.zeros_like(n_ref)
        m_ref[...] = jnp.zeros_like(m_ref)

    L = REC_CHUNK
    r_i = lax.broadcasted_iota(jnp.int32, (L, L), 0)
    c_i = lax.broadcasted_iota(jnp.int32, (L, L), 1)
    upto = (r_i <= c_i).astype(BF16)
    causal = c_i <= r_i

    def chunk(c, carry):
        rs = pl.ds(pl.multiple_of(c * L, L), L)
        q = q_ref[rs, :]
        k = k_ref[rs, :]
        v = v_ref[rs, :]
        li = li_ref[0, 0, :, rs]
        gf = gf_ref[0, 0, :, rs]
        lf = jnp.minimum(gf, 0.0) - jnp.log1p(jnp.exp(-jnp.abs(gf)))
        hi, mid, lo = _split3(jnp.broadcast_to(lf, (8, L)))
        cum = (_dot(hi, upto) + _dot(mid, upto) + _dot(lo, upto))[0:1, :]
        cum_t = jnp.broadcast_to(cum, (L, L)).T
        m_old = m_ref[...]
        dmat = jnp.where(causal, cum_t - cum + li, NEG_BIG)
        inter = cum_t[:, 0:1] + m_old
        m_t = jnp.maximum(inter, jnp.max(dmat, axis=-1, keepdims=True))
        w = jnp.exp(dmat - m_t) * _dot_nt(q, k)
        carry_w = jnp.exp(inter - m_t)
        num = _dot(w.astype(BF16), v) + carry_w * _dot(q, c_ref[...].astype(BF16))
        den = (jnp.sum(w, axis=-1, keepdims=True)
               + carry_w * jnp.sum(q.astype(F32) * n_ref[...], axis=-1, keepdims=True))
        h_out = num / jnp.maximum(jnp.abs(den), jnp.exp(-m_t))

        last = cum[:, L - 1:L]
        g_row = last - cum + li
        m_new = jnp.maximum(last + m_old, jnp.max(g_row, axis=-1, keepdims=True))
        wk = jnp.exp(g_row - m_new)
        kw = k.astype(F32) * jnp.broadcast_to(wk, (L, L)).T
        decay = jnp.exp(last + m_old - m_new)
        c_ref[...] = decay * c_ref[...] + _dot(kw.T.astype(BF16), v)
        n_ref[...] = decay * n_ref[...] + jnp.sum(kw, axis=0, keepdims=True)
        m_ref[...] = m_new

        ms = jnp.mean(h_out * h_out, axis=-1, keepdims=True)
        hn = h_out * lax.rsqrt(ms + EPS) * gain_ref[...]
        y = (hn + skip_ref[...] * xc_ref[rs, :].astype(F32)) * _silu(z_ref[rs, :].astype(F32))
        o_ref[rs, :] = y.astype(o_ref.dtype)
        return carry

    lax.fori_loop(0, rows // L, chunk, 0)


def _mlstm(q_c, k_c, proj, gates, xc, skip, out_gain, w_mix):
    s = q_c.shape[0]
    h = w_mix // HEAD_DIM
    rows = min(REC_ROWS, s)
    gates4 = gates.reshape(2, h, 1, s)
    hd = lambda: pl.BlockSpec((rows, HEAD_DIM), lambda hh, c: (c, hh))
    return pl.pallas_call(
        functools.partial(_mlstm_kernel, rows=rows),
        grid=(h, s // rows),
        in_specs=[hd(), hd(),
                  pl.BlockSpec((rows, HEAD_DIM), lambda hh, c: (c, 7 * h + hh)),
                  pl.BlockSpec((1, 1, 1, rows), lambda hh, c: (0, hh, 0, c)),
                  pl.BlockSpec((1, 1, 1, rows), lambda hh, c: (1, hh, 0, c)),
                  hd(),
                  pl.BlockSpec((rows, HEAD_DIM), lambda hh, c: (c, 8 * h + hh)),
                  pl.BlockSpec((1, HEAD_DIM), lambda hh, c: (0, hh)),
                  pl.BlockSpec((1, HEAD_DIM), lambda hh, c: (0, 0))],
        out_specs=hd(),
        out_shape=jax.ShapeDtypeStruct((s, w_mix), BF16),
        scratch_shapes=[pltpu.VMEM((HEAD_DIM, HEAD_DIM), F32),
                        pltpu.VMEM((1, HEAD_DIM), F32),
                        pltpu.VMEM((1, 1), F32)],
        compiler_params=_params("parallel", "arbitrary"),
        name="mlstm",
    )(q_c, k_c, proj, gates4, gates4, xc, proj, skip.reshape(1, w_mix), out_gain.reshape(1, HEAD_DIM))


def _merge_kernel(h_ref, wg_ref, y_ref, wb_ref, o_ref, acc_ref):
    b = pl.program_id(2)
    gate = jax.nn.sigmoid(_dot(h_ref[...], wg_ref[0]))
    term = gate * _dot(y_ref[0], wb_ref[0])

    @pl.when(b == 0)
    def _():
        acc_ref[...] = term

    @pl.when(b > 0)
    def _():
        acc_ref[...] += term

    @pl.when(b == pl.num_programs(2) - 1)
    def _():
        o_ref[...] = acc_ref[...].astype(o_ref.dtype)


def _merge(hn, w_gate, ys, w_branch):
    s, d = hn.shape
    nb, w_mix = ys.shape[0], ys.shape[2]
    bm, bn = min(ROW_TILE, s), min(COL_TILE, d)
    return pl.pallas_call(
        _merge_kernel,
        grid=(s // bm, d // bn, nb),
        in_specs=[pl.BlockSpec((bm, d), lambda i, j, b: (i, 0)),
                  pl.BlockSpec((1, d, bn), lambda i, j, b: (b, 0, j)),
                  pl.BlockSpec((1, bm, w_mix), lambda i, j, b: (b, i, 0)),
                  pl.BlockSpec((1, w_mix, bn), lambda i, j, b: (b, 0, j))],
        out_specs=pl.BlockSpec((bm, bn), lambda i, j, b: (i, j)),
        out_shape=jax.ShapeDtypeStruct((s, d), BF16),
        scratch_shapes=[pltpu.VMEM((bm, bn), F32)],
        compiler_params=_params("parallel", "arbitrary", "arbitrary"),
        name="merge",
    )(hn, w_gate, ys, w_branch)


def kernel(x, ffn1_norm, ffn1_w_gate, ffn1_w_up, ffn1_w_down, mix_norm, w_in, sb_q_gain, sb_k_gain,
           hg_lb_logits, hg_out_gain, ml_conv_w, ml_conv_b, ml_w_q, ml_w_k, ml_w_if, ml_b_if,
           ml_out_gain, ml_skip, w_merge_gate, w_branch_a, w_branch_b, w_branch_c, w_out,
           ffn2_norm, ffn2_w_gate, ffn2_w_up, ffn2_w_down):
    batch, seq, d_model = x.shape
    depth = w_in.shape[0]
    w_mix = w_branch_a.shape[1]
    n_heads = w_mix // HEAD_DIM
    scale = HEAD_DIM ** -0.5
    bf = lambda a: a.astype(BF16)

    def ffn(xs, norm, w_gate, w_up, w_down):
        n = _rmsnorm_cast(xs, norm)
        hid = _ffn_up(n, bf(w_gate), bf(w_up))
        return _mm_residual(hid, bf(w_down), xs, 0.5)

    outs = []
    for bi in range(batch):
        xs = x[bi]
        for l in range(depth):
            xs = ffn(xs, ffn1_norm[l], ffn1_w_gate[l], ffn1_w_up[l], ffn1_w_down[l])

            hn = _rmsnorm_cast(xs, mix_norm[l])
            proj, f_pre = _in_proj(hn, bf(w_in[l]), w_mix)

            gains = jnp.stack([sb_q_gain[l] * scale, sb_k_gain[l]]).reshape(2, 1, HEAD_DIM)
            qk = _qk_norm(proj, gains, w_mix)
            y_a = _sb_attention(qk, proj, w_mix)

            y_b = _hgrn2(f_pre, proj, hg_lb_logits[:, :], hg_out_gain[l], w_mix, l)

            xc, q_c, k_c, gates = _mlstm_pre(proj, ml_conv_w[l], ml_conv_b[l], bf(ml_w_q[l]), bf(ml_w_k[l]),
                                             bf(ml_w_if[l].T), ml_b_if[l], w_mix)
            y_c = _mlstm(q_c, k_c, proj, gates, xc, ml_skip[l], ml_out_gain[l], w_mix)

            merged = _merge(hn, bf(w_merge_gate[l]), jnp.stack([y_a, y_b, y_c]),
                            bf(jnp.stack([w_branch_a[l], w_branch_b[l], w_branch_c[l]])))
            xs = _mm_residual(merged, bf(w_out[l]), xs, 1.0)

            xs = ffn(xs, ffn2_norm[l], ffn2_w_gate[l], ffn2_w_up[l], ffn2_w_down[l])
        outs.append(xs)
    return jnp.stack(outs)
```

```python
import functools

import jax
import jax.numpy as jnp
from jax import lax
from jax.experimental import pallas as pl
from jax.experimental.pallas import tpu as pltpu

F32 = jnp.float32
BF16 = jnp.bfloat16

HEAD_DIM = 128
EPS = 1e-6
NEG_BIG = -1e30
LB_FLOOR = 1e-30
LOG2_E = 1.4426950408889634

V7X_VMEM_LIMIT_BYTES = 56 * 1024 * 1024
ROW_TILE = 1024
COL_TILE = 512
ATT_TILE = 256
REC_CHUNK = 128
REC_ROWS = 512
SUB = 16
HGRN_HEADS_PER_STEP = 2
MLSTM_HEADS_PER_STEP = 4
ATT_HEADS_PER_STEP = 4


def _params(*sem):
    return pltpu.CompilerParams(dimension_semantics=sem, vmem_limit_bytes=V7X_VMEM_LIMIT_BYTES)


def _dot(a, b):
    return jnp.dot(a, b, preferred_element_type=F32)


def _dot_nt(a, b):
    return lax.dot_general(a, b, (((1,), (1,)), ((), ())), preferred_element_type=F32)


def _split3(x):
    hi = x.astype(BF16)
    r1 = x - hi.astype(F32)
    mid = r1.astype(BF16)
    lo = (r1 - mid.astype(F32)).astype(BF16)
    return hi, mid, lo


def _softplus(z):
    return jnp.maximum(z, 0.0) + jnp.log(1.0 + jnp.exp(-jnp.abs(z)))


def _log_sigmoid(x):
    return jnp.minimum(x, 0.0) - jnp.log(1.0 + jnp.exp(-jnp.abs(x)))


def _silu(x):
    return x * jax.nn.sigmoid(x)


def _hsl(hh):
    return slice(hh * HEAD_DIM, (hh + 1) * HEAD_DIM)


def _rmsnorm_kernel(x_ref, g_ref, o_ref):
    x = x_ref[...]
    ms = jnp.mean(x * x, axis=-1, keepdims=True)
    o_ref[...] = (x * lax.rsqrt(ms + EPS) * g_ref[...]).astype(o_ref.dtype)


def _rmsnorm_cast(x, gains, layer):
    s, d = x.shape
    bm = min(256, s)
    return pl.pallas_call(
        _rmsnorm_kernel,
        grid=(s // bm,),
        in_specs=[pl.BlockSpec((bm, d), lambda i: (i, 0)),
                  pl.BlockSpec((None, 1, d), lambda i: (layer, 0, 0))],
        out_specs=pl.BlockSpec((bm, d), lambda i: (i, 0)),
        out_shape=jax.ShapeDtypeStruct((s, d), BF16),
        compiler_params=_params("parallel"),
        name="rmsnorm_cast",
    )(x, gains.reshape(gains.shape[0], 1, d))


def _ffn_up_kernel(n_ref, wg_ref, wu_ref, o_ref):
    n = n_ref[...]
    g = _dot(n, wg_ref[...])
    u = _dot(n, wu_ref[...])
    o_ref[...] = (_silu(g) * u).astype(o_ref.dtype)


def _ffn_up(n, wg, wu, layer):
    s, d = n.shape
    f = wg.shape[2]
    bm, bn = min(ROW_TILE, s), min(COL_TILE, f)
    wspec = pl.BlockSpec((None, d, bn), lambda i, j: (layer, 0, j))
    return pl.pallas_call(
        _ffn_up_kernel,
        grid=(s // bm, f // bn),
        in_specs=[pl.BlockSpec((bm, d), lambda i, j: (i, 0)), wspec, wspec],
        out_specs=pl.BlockSpec((bm, bn), lambda i, j: (i, j)),
        out_shape=jax.ShapeDtypeStruct((s, f), BF16),
        compiler_params=_params("parallel", "arbitrary"),
        name="ffn_up",
    )(n, wg, wu)


def _mm_res_kernel(a_ref, w_ref, x_ref, o_ref, *, scale):
    o_ref[...] = x_ref[...] + scale * _dot(a_ref[...], w_ref[...])


def _mm_residual(a, w, layer, x, scale):
    s, k = a.shape
    n = w.shape[2]
    bm, bn = min(ROW_TILE, s), min(COL_TILE, n)
    return pl.pallas_call(
        functools.partial(_mm_res_kernel, scale=scale),
        grid=(s // bm, n // bn),
        in_specs=[pl.BlockSpec((bm, k), lambda i, j: (i, 0)),
                  pl.BlockSpec((None, k, bn), lambda i, j: (layer, 0, j)),
                  pl.BlockSpec((bm, bn), lambda i, j: (i, j))],
        out_specs=pl.BlockSpec((bm, bn), lambda i, j: (i, j)),
        out_shape=jax.ShapeDtypeStruct((s, n), F32),
        compiler_params=_params("parallel", "arbitrary"),
        name="mm_residual",
    )(a, w, x)


def _mm_kernel(a_ref, w_ref, o_ref):
    o_ref[...] = _dot(a_ref[...], w_ref[...]).astype(o_ref.dtype)


def _mm(a, w, layer, n_out, bn, col_block_of, out_dtype):
    s, k = a.shape
    bm = min(ROW_TILE, s)
    return pl.pallas_call(
        _mm_kernel,
        grid=(s // bm, n_out // bn),
        in_specs=[pl.BlockSpec((bm, k), lambda i, j: (i, 0)),
                  pl.BlockSpec((None, k, bn), lambda i, j: (layer, 0, col_block_of(j)))],
        out_specs=pl.BlockSpec((bm, bn), lambda i, j: (i, j)),
        out_shape=jax.ShapeDtypeStruct((s, n_out), out_dtype),
        compiler_params=_params("parallel", "arbitrary"),
        name="in_proj",
    )(a, w)


def _in_proj(hn, w_in, layer, w_mix):
    bn = min(COL_TILE, w_mix)
    nblk = w_mix // bn
    proj = _mm(hn, w_in, layer, 9 * w_mix, bn, lambda j: j + jnp.where(j >= 3 * nblk, nblk, 0), BF16)
    f_pre = _mm(hn, w_in, layer, w_mix, bn, lambda j: j + 3 * nblk, F32)
    return proj, f_pre


def _qk_norm_kernel(x_ref, g_ref, o_ref, *, n_heads):
    g = g_ref[...]
    for h in range(n_heads):
        x = x_ref[:, _hsl(h)].astype(F32)
        ms = jnp.mean(x * x, axis=-1, keepdims=True)
        o_ref[:, _hsl(h)] = (x * lax.rsqrt(ms + EPS) * g).astype(o_ref.dtype)


def _qk_norm(proj, gains, layer, w_mix):
    s = proj.shape[0]
    bm = min(512, s)
    return pl.pallas_call(
        functools.partial(_qk_norm_kernel, n_heads=w_mix // HEAD_DIM),
        grid=(s // bm, 2),
        in_specs=[pl.BlockSpec((bm, w_mix), lambda i, g: (i, g)),
                  pl.BlockSpec((None, None, 1, HEAD_DIM), lambda i, g: (layer, g, 0, 0))],
        out_specs=pl.BlockSpec((bm, w_mix), lambda i, g: (i, g)),
        out_shape=jax.ShapeDtypeStruct((s, 2 * w_mix), BF16),
        compiler_params=_params("parallel", "arbitrary"),
        name="qk_norm",
    )(proj, gains)


def _sb_kernel(q_ref, k_ref, v_ref, o_ref, *, tile, hps):
    qi = pl.program_id(1)
    row = lax.broadcasted_iota(jnp.int32, (tile, tile), 0)
    col = lax.broadcasted_iota(jnp.int32, (tile, tile), 1)
    later = (row > col).astype(BF16)
    causal = col < row

    def block(kb, carry, masked):
        off = pl.multiple_of(kb * tile, tile)
        heads = range(hps)
        nzs = [_dot_nt(q_ref[:, _hsl(hh)], k_ref[pl.ds(off, tile), _hsl(hh)]) for hh in heads]

        def stage_softplus(hh):
            nz = nzs[hh]
            neg_abs = lax.bitcast_convert_type(
                lax.bitcast_convert_type(nz, jnp.uint32) | jnp.uint32(0x80000000), F32)
            log_1m = jnp.minimum(nz, 0.0) - jnp.log2(1.0 + jnp.exp2(neg_abs))
            log_beta = log_1m - nz
            if masked:
                log_1m = jnp.where(causal, log_1m, 0.0)
            between = _dot(log_1m.astype(BF16), later)
            return log_beta, between, jnp.sum(log_1m, axis=-1, keepdims=True)

        def stage_out(hh, log_beta, between, total):
            acc, run = carry[hh]
            att = jnp.exp2(log_beta + between)
            if masked:
                att = jnp.where(causal, att, 0.0)
            acc = acc + jnp.exp2(run) * _dot(att.astype(BF16), v_ref[pl.ds(off, tile), _hsl(hh)])
            return acc, run + total

        mids = [stage_softplus(hh) for hh in heads]
        return tuple(stage_out(hh, *mids[hh]) for hh in heads)

    init = tuple((jnp.zeros((tile, HEAD_DIM), F32), jnp.zeros((tile, 1), F32)) for _ in range(hps))
    carry = block(qi, init, True)
    carry = lax.fori_loop(0, qi, lambda it, c: block(qi - 1 - it, c, False), carry)
    for hh in range(hps):
        o_ref[:, _hsl(hh)] = carry[hh][0].astype(o_ref.dtype)


def _sb_attention(qk, proj, w_mix):
    s = qk.shape[0]
    h = w_mix // HEAD_DIM
    hps = min(ATT_HEADS_PER_STEP, h)
    hg = h // hps
    wide = hps * HEAD_DIM
    tile = min(ATT_TILE, s)
    return pl.pallas_call(
        functools.partial(_sb_kernel, tile=tile, hps=hps),
        grid=(hg, s // tile),
        in_specs=[pl.BlockSpec((tile, wide), lambda g, i: (i, g)),
                  pl.BlockSpec((s, wide), lambda g, i: (0, hg + g)),
                  pl.BlockSpec((s, wide), lambda g, i: (0, 2 * hg + g))],
        out_specs=pl.BlockSpec((tile, wide), lambda g, i: (i, g)),
        out_shape=jax.ShapeDtypeStruct((s, w_mix), BF16),
        compiler_params=_params("parallel", "arbitrary"),
        name="sb_attention",
    )(qk, qk, proj)


def _hgrn2_kernel(f_ref, v_ref, q_ref, g_ref, lbl_ref, gain_ref, o_ref, st_ref, *, layer, rows, hps):
    @pl.when(pl.program_id(1) == 0)
    def _():
        st_ref[...] = jnp.zeros_like(st_ref)

    logits = lbl_ref[...]
    e = jnp.exp(logits - jnp.max(logits, axis=0, keepdims=True))
    p = e / jnp.sum(e, axis=0, keepdims=True)
    lb_all = jnp.zeros((1, hps * HEAD_DIM), F32)
    for i in range(1, layer + 1):
        lb_all = lb_all + p[i:i + 1, :]

    L = REC_CHUNK
    r_i = lax.broadcasted_iota(jnp.int32, (L, L), 0)
    c_i = lax.broadcasted_iota(jnp.int32, (L, L), 1)
    tri_incl = (c_i <= r_i).astype(BF16)
    half_row = lax.broadcasted_iota(jnp.int32, (SUB // 2, HEAD_DIM), 0)

    def chunk(c, carry):
        rs = pl.ds(pl.multiple_of(c * L, L), L)
        for hh in range(hps):
            lb = lb_all[:, _hsl(hh)]
            log_lb = jnp.log(jnp.maximum(lb, LB_FLOOR))
            x = f_ref[rs, _hsl(hh)]
            b = jnp.log1p(-lb) + _log_sigmoid(x)
            lf = jnp.maximum(log_lb, b) + jnp.log(1.0 + jnp.exp(-jnp.abs(log_lb - b)))
            kk = (1.0 - lb) * jax.nn.sigmoid(-x)
            qq = _silu(q_ref[rs, _hsl(hh)].astype(F32))
            v = v_ref[rs, _hsl(hh)]
            vf = v.astype(F32)

            hi, mid, lo = _split3(lf)
            cum = (_dot(tri_incl, hi) + _dot(tri_incl, mid) + _dot(tri_incl, lo)) * LOG2_E
            last = cum[L - 1:L, :]

            st = st_ref[hh]
            out = _dot_nt((qq * jnp.exp2(cum)).astype(BF16), st.astype(BF16))

            a_rows = [jnp.zeros((SUB, L), F32)]
            for i in range(1, L // SUB):
                lo_r, hi_r = i * SUB, (i + 1) * SUB
                anchor = cum[lo_r - 1:lo_r, :]
                qt = qq[lo_r:hi_r, :] * jnp.exp2(cum[lo_r:hi_r, :] - anchor)
                kt = kk[:lo_r, :] * jnp.exp2(anchor - cum[:lo_r, :])
                kt = jnp.concatenate([kt, jnp.zeros((L - lo_r, HEAD_DIM), F32)], axis=0)
                a_rows.append(_dot_nt(qt.astype(BF16), kt.astype(BF16)))
            a_off = jnp.concatenate(a_rows, axis=0)
            out = out + _dot(a_off.astype(BF16), v)

            diag = []
            for i in range(L // SUB):
                base = i * SUB
                halves = []
                for r0 in (0, SUB // 2):
                    rows_r = slice(base + r0, base + r0 + SUB // 2)
                    cb, qb = cum[rows_r, :], qq[rows_r, :]
                    o_half = jnp.zeros((SUB // 2, HEAD_DIM), F32)
                    for s_ in range(r0 + SUB // 2):
                        srow = slice(base + s_, base + s_ + 1)
                        diff = cb - cum[srow, :]
                        if s_ > r0:
                            diff = jnp.where(half_row >= s_ - r0, diff, NEG_BIG)
                        a_col = jnp.sum(qb * jnp.exp2(diff) * kk[srow, :], axis=-1, keepdims=True)
                        o_half = o_half + a_col * vf[srow, :]
                    halves.append(o_half)
                diag.extend(halves)
            out = out + jnp.concatenate(diag, axis=0)

            kt = kk * jnp.exp2(last - cum)
            st_ref[hh] = jnp.exp2(last) * st + _dot(vf.T.astype(BF16), kt.astype(BF16))

            ms = jnp.mean(out * out, axis=-1, keepdims=True)
            y = out * lax.rsqrt(ms + EPS) * gain_ref[...]
            o_ref[rs, _hsl(hh)] = (y * _silu(g_ref[rs, _hsl(hh)].astype(F32))).astype(o_ref.dtype)
        return carry

    lax.fori_loop(0, rows // L, chunk, 0)


def _hgrn2(f_pre, proj, lb_logits, out_gain, w_mix, layer):
    s = f_pre.shape[0]
    h = w_mix // HEAD_DIM
    hps = min(HGRN_HEADS_PER_STEP, h)
    hg = h // hps
    wide = hps * HEAD_DIM
    rows = min(REC_ROWS, s)
    depth = lb_logits.shape[0]
    blk = lambda grp: pl.BlockSpec((rows, wide), lambda g, c: (c, grp * hg + g))
    return pl.pallas_call(
        functools.partial(_hgrn2_kernel, layer=layer, rows=rows, hps=hps),
        grid=(hg, s // rows),
        in_specs=[pl.BlockSpec((rows, wide), lambda g, c: (c, g)),
                  blk(3), blk(4), blk(5),
                  pl.BlockSpec((depth, wide), lambda g, c: (0, g)),
                  pl.BlockSpec((None, 1, HEAD_DIM), lambda g, c: (layer, 0, 0))],
        out_specs=pl.BlockSpec((rows, wide), lambda g, c: (c, g)),
        out_shape=jax.ShapeDtypeStruct((s, w_mix), BF16),
        scratch_shapes=[pltpu.VMEM((hps, HEAD_DIM, HEAD_DIM), F32)],
        compiler_params=_params("parallel", "arbitrary"),
        name="hgrn2",
    )(f_pre, proj, proj, proj, lb_logits, out_gain.reshape(out_gain.shape[0], 1, HEAD_DIM))


def _mlstm_pre_kernel(x_ref, prev_ref, v_ref, cw_ref, cb_ref, wq_ref, wk_ref, wif_ref, bif_ref,
                      xc_ref, q_ref, k_ref, gates_ref, *, n_heads, conv_k):
    i = pl.program_id(0)
    x = x_ref[...].astype(F32)
    prev = prev_ref[8:16, :].astype(F32)
    prev = jnp.where(i > 0, prev, 0.0)
    row8 = lax.broadcasted_iota(jnp.int32, prev.shape, 0)
    acc = x * cw_ref[conv_k - 1:conv_k, :] + cb_ref[...]
    for back in range(1, conv_k):
        sh = pltpu.roll(x, back, 0)
        head = jnp.where(row8 < back, pltpu.roll(prev, back, 0), sh[:8, :])
        sh = jnp.concatenate([head, sh[8:, :]], axis=0)
        acc = acc + sh * cw_ref[conv_k - 1 - back:conv_k - back, :]
    xc = _silu(acc)
    xc_ref[...] = xc.astype(xc_ref.dtype)
    xcb = xc.astype(BF16)
    scale = HEAD_DIM ** -0.5
    w_mix = n_heads * HEAD_DIM
    g = jnp.zeros(gates_ref.shape, F32) + bif_ref[...]
    for h in range(n_heads):
        qh = _dot(xcb[:, _hsl(h)], wq_ref[h])
        kh = _dot(xcb[:, _hsl(h)], wk_ref[h])
        q_ref[:, _hsl(h)] = qh.astype(q_ref.dtype)
        k_ref[:, _hsl(h)] = (kh * scale).astype(k_ref.dtype)
        g = g + _dot_nt(wif_ref[:, _hsl(h)], qh.astype(BF16))
        g = g + _dot_nt(wif_ref[:, w_mix + h * HEAD_DIM:w_mix + (h + 1) * HEAD_DIM], kh.astype(BF16))
    g = g + _dot_nt(wif_ref[:, 2 * w_mix:], v_ref[...])
    gates_ref[...] = g


def _mlstm_pre(proj, conv_w, conv_b, w_q, w_k, w_if_t, b_if, layer, w_mix):
    s = proj.shape[0]
    h = w_mix // HEAD_DIM
    bm = min(512, s)
    conv_k = conv_w.shape[1]
    nl = conv_w.shape[0]
    lay3 = lambda i: (layer, 0, 0)
    return pl.pallas_call(
        functools.partial(_mlstm_pre_kernel, n_heads=h, conv_k=conv_k),
        grid=(s // bm,),
        in_specs=[pl.BlockSpec((bm, w_mix), lambda i: (i, 6)),
                  pl.BlockSpec((16, w_mix), lambda i: (jnp.maximum(i * (bm // 16) - 1, 0), 6)),
                  pl.BlockSpec((bm, w_mix), lambda i: (i, 7)),
                  pl.BlockSpec((None, conv_k, w_mix), lay3),
                  pl.BlockSpec((None, 1, w_mix), lay3),
                  pl.BlockSpec((None, h, HEAD_DIM, HEAD_DIM), lambda i: (layer, 0, 0, 0)),
                  pl.BlockSpec((None, h, HEAD_DIM, HEAD_DIM), lambda i: (layer, 0, 0, 0)),
                  pl.BlockSpec((None, 2 * h, 3 * w_mix), lay3),
                  pl.BlockSpec((None, 2 * h, 1), lay3)],
        out_specs=[pl.BlockSpec((bm, w_mix), lambda i: (i, 0)),
                   pl.BlockSpec((bm, w_mix), lambda i: (i, 0)),
                   pl.BlockSpec((bm, w_mix), lambda i: (i, 0)),
                   pl.BlockSpec((2 * h, bm), lambda i: (0, i))],
        out_shape=[jax.ShapeDtypeStruct((s, w_mix), BF16),
                   jax.ShapeDtypeStruct((s, w_mix), BF16),
                   jax.ShapeDtypeStruct((s, w_mix), BF16),
                   jax.ShapeDtypeStruct((2 * h, s), F32)],
        compiler_params=_params("parallel"),
        name="mlstm_pre",
    )(proj, proj, proj, conv_w, conv_b.reshape(nl, 1, w_mix), w_q, w_k, w_if_t, b_if.reshape(nl, 2 * h, 1))


def _mlstm_kernel(q_ref, k_ref, v_ref, li_ref, gf_ref, xc_ref, z_ref, skip_ref, gain_ref,
                  o_ref, c_ref, n_ref, m_ref, *, rows, hps):
    @pl.when(pl.program_id(1) == 0)
    def _():
        c_ref[...] = jnp.zeros_like(c_ref)
        n_ref[...] = jnp.zeros_like(n_ref)
        m_ref[...] = jnp.zeros_like(m_ref)

    L = REC_CHUNK
    r_i = lax.broadcasted_iota(jnp.int32, (L, L), 0)
    c_i = lax.broadcasted_iota(jnp.int32, (L, L), 1)
    upto = (r_i <= c_i).astype(BF16)
    causal = c_i <= r_i

    def chunk(c, carry):
        rs = pl.ds(pl.multiple_of(c * L, L), L)
        heads = range(hps)

        def stage_gates(hh):
            li = li_ref[0, hh, :, rs]
            lf = _log_sigmoid(gf_ref[0, hh, :, rs])
            hi, mid, lo = _split3(jnp.broadcast_to(lf, (8, L)))
            cum = (_dot(hi, upto) + _dot(mid, upto) + _dot(lo, upto))[0:1, :]
            return li, cum

        def stage_weights(hh, li, cum):
            q = q_ref[rs, _hsl(hh)]
            k = k_ref[rs, _hsl(hh)]
            cum_t = jnp.broadcast_to(cum, (L, L)).T
            m_old = m_ref[hh]
            dmat = jnp.where(causal, cum_t - cum + li, NEG_BIG)
            inter = cum_t[:, 0:1] + m_old
            m_t = jnp.maximum(inter, jnp.max(dmat, axis=-1, keepdims=True))
            w = jnp.exp(dmat - m_t) * _dot_nt(q, k)
            return w, jnp.exp(inter - m_t), m_t

        def stage_output(hh, w, carry_w, m_t):
            q = q_ref[rs, _hsl(hh)]
            v = v_ref[rs, _hsl(hh)]
            num = _dot(w.astype(BF16), v) + carry_w * _dot(q, c_ref[hh].astype(BF16))
            den = (jnp.sum(w, axis=-1, keepdims=True)
                   + carry_w * jnp.sum(q.astype(F32) * n_ref[hh], axis=-1, keepdims=True))
            h_out = num / jnp.maximum(jnp.abs(den), jnp.exp(-m_t))
            ms = jnp.mean(h_out * h_out, axis=-1, keepdims=True)
            hn = h_out * lax.rsqrt(ms + EPS) * gain_ref[...]
            y = ((hn + skip_ref[:, _hsl(hh)] * xc_ref[rs, _hsl(hh)].astype(F32))
                 * _silu(z_ref[rs, _hsl(hh)].astype(F32)))
            o_ref[rs, _hsl(hh)] = y.astype(o_ref.dtype)

        def stage_state(hh, li, cum):
            k = k_ref[rs, _hsl(hh)]
            v = v_ref[rs, _hsl(hh)]
            m_old = m_ref[hh]
            last = cum[:, L - 1:L]
            g_row = last - cum + li
            m_new = jnp.maximum(last + m_old, jnp.max(g_row, axis=-1, keepdims=True))
            wk = jnp.exp(g_row - m_new)
            kw = k.astype(F32) * jnp.broadcast_to(wk, (L, L)).T
            decay = jnp.exp(last + m_old - m_new)
            c_ref[hh] = decay * c_ref[hh] + _dot(kw.T.astype(BF16), v)
            n_ref[hh] = decay * n_ref[hh] + jnp.sum(kw, axis=0, keepdims=True)
            m_ref[hh] = m_new

        gates = [stage_gates(hh) for hh in heads]
        weights = [stage_weights(hh, *gates[hh]) for hh in heads]
        for hh in heads:
            stage_output(hh, *weights[hh])
        for hh in heads:
            stage_state(hh, *gates[hh])
        return carry

    lax.fori_loop(0, rows // L, chunk, 0)


def _mlstm(q_c, k_c, proj, gates, xc, skip, out_gain, layer, w_mix):
    s = q_c.shape[0]
    h = w_mix // HEAD_DIM
    hps = min(MLSTM_HEADS_PER_STEP, h)
    hg = h // hps
    wide = hps * HEAD_DIM
    rows = min(REC_ROWS, s)
    nl = skip.shape[0]
    gates4 = gates.reshape(2, h, 1, s)
    hd = lambda: pl.BlockSpec((rows, wide), lambda g, c: (c, g))
    return pl.pallas_call(
        functools.partial(_mlstm_kernel, rows=rows, hps=hps),
        grid=(hg, s // rows),
        in_specs=[hd(), hd(),
                  pl.BlockSpec((rows, wide), lambda g, c: (c, 7 * hg + g)),
                  pl.BlockSpec((1, hps, 1, rows), lambda g, c: (0, g, 0, c)),
                  pl.BlockSpec((1, hps, 1, rows), lambda g, c: (1, g, 0, c)),
                  hd(),
                  pl.BlockSpec((rows, wide), lambda g, c: (c, 8 * hg + g)),
                  pl.BlockSpec((None, 1, wide), lambda g, c: (layer, 0, g)),
                  pl.BlockSpec((None, 1, HEAD_DIM), lambda g, c: (layer, 0, 0))],
        out_specs=hd(),
        out_shape=jax.ShapeDtypeStruct((s, w_mix), BF16),
        scratch_shapes=[pltpu.VMEM((hps, HEAD_DIM, HEAD_DIM), F32),
                        pltpu.VMEM((hps, 1, HEAD_DIM), F32),
                        pltpu.VMEM((hps, 1, 1), F32)],
        compiler_params=_params("parallel", "arbitrary"),
        name="mlstm",
    )(q_c, k_c, proj, gates4, gates4, xc, proj, skip.reshape(nl, 1, w_mix),
      out_gain.reshape(nl, 1, HEAD_DIM))


def _merge_kernel(h_ref, wg_ref, ya_ref, yb_ref, yc_ref, wb_ref, o_ref, acc_ref):
    b = pl.program_id(2)
    gate = jax.nn.sigmoid(_dot(h_ref[...], wg_ref[...]))

    @pl.when(b == 0)
    def _():
        acc_ref[...] = gate * _dot(ya_ref[...], wb_ref[...])

    @pl.when(b == 1)
    def _():
        acc_ref[...] += gate * _dot(yb_ref[...], wb_ref[...])

    @pl.when(b == 2)
    def _():
        o_ref[...] = (acc_ref[...] + gate * _dot(yc_ref[...], wb_ref[...])).astype(o_ref.dtype)


def _merge(hn, w_gate, ys, w_branch, layer):
    s, d = hn.shape
    w_mix = ys[0].shape[1]
    bm, bn = min(ROW_TILE, s), min(COL_TILE, d)
    yspec = pl.BlockSpec((bm, w_mix), lambda i, j, b: (i, 0))
    return pl.pallas_call(
        _merge_kernel,
        grid=(s // bm, d // bn, 3),
        in_specs=[pl.BlockSpec((bm, d), lambda i, j, b: (i, 0)),
                  pl.BlockSpec((None, None, d, bn), lambda i, j, b: (layer, b, 0, j)),
                  yspec, yspec, yspec,
                  pl.BlockSpec((None, None, w_mix, bn), lambda i, j, b: (b, layer, 0, j))],
        out_specs=pl.BlockSpec((bm, bn), lambda i, j, b: (i, j)),
        out_shape=jax.ShapeDtypeStruct((s, d), BF16),
        scratch_shapes=[pltpu.VMEM((bm, bn), F32)],
        compiler_params=_params("parallel", "arbitrary", "arbitrary"),
        name="merge",
    )(hn, w_gate, *ys, w_branch)


def kernel(x, ffn1_norm, ffn1_w_gate, ffn1_w_up, ffn1_w_down, mix_norm, w_in, sb_q_gain, sb_k_gain,
           hg_lb_logits, hg_out_gain, ml_conv_w, ml_conv_b, ml_w_q, ml_w_k, ml_w_if, ml_b_if,
           ml_out_gain, ml_skip, w_merge_gate, w_branch_a, w_branch_b, w_branch_c, w_out,
           ffn2_norm, ffn2_w_gate, ffn2_w_up, ffn2_w_down):
    batch, seq, d_model = x.shape
    depth = w_in.shape[0]
    w_mix = w_branch_a.shape[1]
    scale = HEAD_DIM ** -0.5
    bf = lambda a: a.astype(BF16)

    f1g, f1u, f1d = bf(ffn1_w_gate), bf(ffn1_w_up), bf(ffn1_w_down)
    f2g, f2u, f2d = bf(ffn2_w_gate), bf(ffn2_w_up), bf(ffn2_w_down)
    w_in_b, w_gate_b, w_out_b = bf(w_in), bf(w_merge_gate), bf(w_out)
    w_branch_b3 = jnp.stack([bf(w_branch_a), bf(w_branch_b), bf(w_branch_c)])
    wq_b, wk_b, wif_t = bf(ml_w_q), bf(ml_w_k), bf(jnp.swapaxes(ml_w_if, 1, 2))
    qk_gains = jnp.stack([sb_q_gain * (-scale * LOG2_E), sb_k_gain], axis=1).reshape(depth, 2, 1, HEAD_DIM)

    def ffn(xs, l, norm, w_gate, w_up, w_down):
        n = _rmsnorm_cast(xs, norm, l)
        hid = _ffn_up(n, w_gate, w_up, l)
        return _mm_residual(hid, w_down, l, xs, 0.5)

    outs = []
    for bi in range(batch):
        xs = x[bi]
        for l in range(depth):
            xs = ffn(xs, l, ffn1_norm, f1g, f1u, f1d)

            hn = _rmsnorm_cast(xs, mix_norm, l)
            proj, f_pre = _in_proj(hn, w_in_b, l, w_mix)
            qk = _qk_norm(proj, qk_gains, l, w_mix)
            y_a = _sb_attention(qk, proj, w_mix)
            y_b = _hgrn2(f_pre, proj, hg_lb_logits, hg_out_gain, w_mix, l)
            xc, q_c, k_c, gates = _mlstm_pre(proj, ml_conv_w, ml_conv_b, wq_b, wk_b, wif_t, ml_b_if, l, w_mix)
            y_c = _mlstm(q_c, k_c, proj, gates, xc, ml_skip, ml_out_gain, l, w_mix)
            merged = _merge(hn, w_gate_b, (y_a, y_b, y_c), w_branch_b3, l)
            xs = _mm_residual(merged, w_out_b, l, xs, 1.0)

            xs = ffn(xs, l, ffn2_norm, f2g, f2u, f2d)
        outs.append(xs)
    return outs[0][None] if batch == 1 else jnp.stack(outs)
```

```python
import functools

import jax
import jax.numpy as jnp
from jax import lax
from jax.experimental import pallas as pl
from jax.experimental.pallas import tpu as pltpu

F32 = jnp.float32
BF16 = jnp.bfloat16

HEAD_DIM = 128
EPS = 1e-6
NEG_BIG = -1e30
LB_FLOOR = 1e-30
LOG2_E = 1.4426950408889634

V7X_VMEM_LIMIT_BYTES = 56 * 1024 * 1024
ROW_TILE = 1024
COL_TILE = 512
PROJ_COL_TILE = 1024
MERGE_COL_TILE = 256
SSQ_LANES = 128
ATT_TILE = 256
ATT_SLAB = 16
REC_CHUNK = 128
REC_ROWS = 512
SUB = 16
HGRN_HEADS_PER_STEP = 2
MLSTM_HEADS_PER_STEP = 4
ATT_HEADS_PER_STEP = 4


def _params(*sem):
    return pltpu.CompilerParams(dimension_semantics=sem, vmem_limit_bytes=V7X_VMEM_LIMIT_BYTES)


def _dot(a, b):
    return jnp.dot(a, b, preferred_element_type=F32)


def _dot_nt(a, b):
    return lax.dot_general(a, b, (((1,), (1,)), ((), ())), preferred_element_type=F32)


def _split3(x):
    hi = x.astype(BF16)
    r1 = x - hi.astype(F32)
    mid = r1.astype(BF16)
    lo = (r1 - mid.astype(F32)).astype(BF16)
    return hi, mid, lo


def _softplus(z):
    return jnp.maximum(z, 0.0) + jnp.log(1.0 + jnp.exp(-jnp.abs(z)))


def _log_sigmoid(x):
    return jnp.minimum(x, 0.0) - jnp.log(1.0 + jnp.exp(-jnp.abs(x)))


def _silu(x):
    return x * jax.nn.sigmoid(x)


def _hsl(hh):
    return slice(hh * HEAD_DIM, (hh + 1) * HEAD_DIM)


def _row_factor(ssq_ref, d):
    return lax.rsqrt(ssq_ref[:, 0:1] * (1.0 / d) + EPS)


def _norm_prep_kernel(x_ref, g_ref, xg_ref, ssq_ref):
    x = x_ref[...]
    xg_ref[...] = (x * g_ref[...]).astype(xg_ref.dtype)
    ssq_ref[...] = jnp.broadcast_to(jnp.sum(x * x, axis=-1, keepdims=True), ssq_ref.shape)


def _norm_prep(x, gains, layer):
    s, d = x.shape
    bm = min(256, s)
    return pl.pallas_call(
        _norm_prep_kernel,
        grid=(s // bm,),
        in_specs=[pl.BlockSpec((bm, d), lambda i: (i, 0)),
                  pl.BlockSpec((None, 1, d), lambda i: (layer, 0, 0))],
        out_specs=[pl.BlockSpec((bm, d), lambda i: (i, 0)),
                   pl.BlockSpec((bm, SSQ_LANES), lambda i: (i, 0))],
        out_shape=[jax.ShapeDtypeStruct((s, d), BF16),
                   jax.ShapeDtypeStruct((s, SSQ_LANES), F32)],
        compiler_params=_params("parallel"),
        name="norm_prep",
    )(x, gains.reshape(gains.shape[0], 1, d))


def _ffn_up_kernel(xg_ref, ssq_ref, wg_ref, wu_ref, o_ref):
    xg = xg_ref[...]
    r = _row_factor(ssq_ref, xg.shape[1])
    g = r * _dot(xg, wg_ref[...])
    u = r * _dot(xg, wu_ref[...])
    o_ref[...] = (_silu(g) * u).astype(o_ref.dtype)


def _ffn_up(xg, ssq, wg, wu, layer):
    s, d = xg.shape
    f = wg.shape[2]
    bm, bn = min(ROW_TILE, s), min(COL_TILE, f)
    wspec = pl.BlockSpec((None, d, bn), lambda i, j: (layer, 0, j))
    return pl.pallas_call(
        _ffn_up_kernel,
        grid=(s // bm, f // bn),
        in_specs=[pl.BlockSpec((bm, d), lambda i, j: (i, 0)),
                  pl.BlockSpec((bm, SSQ_LANES), lambda i, j: (i, 0)), wspec, wspec],
        out_specs=pl.BlockSpec((bm, bn), lambda i, j: (i, j)),
        out_shape=jax.ShapeDtypeStruct((s, f), BF16),
        compiler_params=_params("parallel", "arbitrary"),
        name="ffn_up",
    )(xg, ssq, wg, wu)


def _mm_res_kernel(a_ref, w_ref, x_ref, o_ref, *, scale):
    o_ref[...] = x_ref[...] + scale * _dot(a_ref[...], w_ref[...])


def _mm_res_norm_kernel(a_ref, w_ref, x_ref, g_ref, o_ref, xg_ref, ssq_ref, *, scale):
    a = a_ref[...]
    bn = o_ref.shape[1]
    half = bn // 2 if bn % 256 == 0 else bn
    part = None
    for c0 in range(0, bn, half):
        cs = slice(c0, c0 + half)
        y = x_ref[:, cs] + scale * _dot(a, w_ref[:, cs])
        o_ref[:, cs] = y
        xg_ref[:, cs] = (y * g_ref[:, cs]).astype(xg_ref.dtype)
        p = jnp.sum(y * y, axis=-1, keepdims=True)
        part = p if part is None else part + p
    part = jnp.broadcast_to(part, ssq_ref.shape)

    @pl.when(pl.program_id(1) == 0)
    def _():
        ssq_ref[...] = part

    @pl.when(pl.program_id(1) > 0)
    def _():
        ssq_ref[...] += part


def _mm_residual(a, w, layer, x, scale, next_gain=None, next_layer=0):
    s, k = a.shape
    n = w.shape[2]
    bm, bn = min(ROW_TILE, s), min(COL_TILE, n)
    in_specs = [pl.BlockSpec((bm, k), lambda i, j: (i, 0)),
                pl.BlockSpec((None, k, bn), lambda i, j: (layer, 0, j)),
                pl.BlockSpec((bm, bn), lambda i, j: (i, j))]
    tile = pl.BlockSpec((bm, bn), lambda i, j: (i, j))
    if next_gain is None:
        return pl.pallas_call(
            functools.partial(_mm_res_kernel, scale=scale),
            grid=(s // bm, n // bn),
            in_specs=in_specs,
            out_specs=tile,
            out_shape=jax.ShapeDtypeStruct((s, n), F32),
            compiler_params=_params("parallel", "arbitrary"),
            name="mm_residual",
        )(a, w, x)
    return pl.pallas_call(
        functools.partial(_mm_res_norm_kernel, scale=scale),
        grid=(s // bm, n // bn),
        in_specs=in_specs + [pl.BlockSpec((None, 1, bn), lambda i, j: (next_layer, 0, j))],
        out_specs=[tile, tile, pl.BlockSpec((bm, SSQ_LANES), lambda i, j: (i, 0))],
        out_shape=[jax.ShapeDtypeStruct((s, n), F32),
                   jax.ShapeDtypeStruct((s, n), BF16),
                   jax.ShapeDtypeStruct((s, SSQ_LANES), F32)],
        compiler_params=_params("parallel", "arbitrary"),
        name="mm_residual_norm",
    )(a, w, x, next_gain.reshape(next_gain.shape[0], 1, n))


def _mm_kernel(xg_ref, ssq_ref, w_ref, o_ref):
    xg = xg_ref[...]
    o_ref[...] = (_row_factor(ssq_ref, xg.shape[1]) * _dot(xg, w_ref[...])).astype(o_ref.dtype)


def _mm(xg, ssq, w, layer, n_out, bn, col_block_of, out_dtype):
    s, k = xg.shape
    bm = min(ROW_TILE, s)
    return pl.pallas_call(
        _mm_kernel,
        grid=(s // bm, n_out // bn),
        in_specs=[pl.BlockSpec((bm, k), lambda i, j: (i, 0)),
                  pl.BlockSpec((bm, SSQ_LANES), lambda i, j: (i, 0)),
                  pl.BlockSpec((None, k, bn), lambda i, j: (layer, 0, col_block_of(j)))],
        out_specs=pl.BlockSpec((bm, bn), lambda i, j: (i, j)),
        out_shape=jax.ShapeDtypeStruct((s, n_out), out_dtype),
        compiler_params=_params("parallel", "arbitrary"),
        name="in_proj",
    )(xg, ssq, w)


def _in_proj(xg, ssq, w_in, layer, w_mix):
    bn = min(PROJ_COL_TILE, w_mix)
    nblk = w_mix // bn
    proj = _mm(xg, ssq, w_in, layer, 9 * w_mix, bn, lambda j: j + jnp.where(j >= 3 * nblk, nblk, 0), BF16)
    f_pre = _mm(xg, ssq, w_in, layer, w_mix, bn, lambda j: j + 3 * nblk, F32)
    return proj, f_pre


def _qk_norm_kernel(x_ref, g_ref, o_ref, *, n_heads):
    g = g_ref[...]
    for h in range(n_heads):
        x = x_ref[:, _hsl(h)].astype(F32)
        ms = jnp.mean(x * x, axis=-1, keepdims=True)
        o_ref[:, _hsl(h)] = (x * lax.rsqrt(ms + EPS) * g).astype(o_ref.dtype)


def _qk_norm(proj, gains, layer, w_mix):
    s = proj.shape[0]
    bm = min(512, s)
    return pl.pallas_call(
        functools.partial(_qk_norm_kernel, n_heads=w_mix // HEAD_DIM),
        grid=(s // bm, 2),
        in_specs=[pl.BlockSpec((bm, w_mix), lambda i, g: (i, g)),
                  pl.BlockSpec((None, None, 1, HEAD_DIM), lambda i, g: (layer, g, 0, 0))],
        out_specs=pl.BlockSpec((bm, w_mix), lambda i, g: (i, g)),
        out_shape=jax.ShapeDtypeStruct((s, 2 * w_mix), BF16),
        compiler_params=_params("parallel", "arbitrary"),
        name="qk_norm",
    )(proj, gains)


def _sb_kernel(q_ref, k_ref, v_ref, o_ref, acc_ref, run_ref, nz_ref, *, tile, hps):
    qi = pl.program_id(1)
    row = lax.broadcasted_iota(jnp.int32, (tile, tile), 0)
    col = lax.broadcasted_iota(jnp.int32, (tile, tile), 1)
    later = (row > col).astype(BF16)
    causal = col < row
    heads = range(hps)

    acc_ref[...] = jnp.zeros_like(acc_ref)
    run_ref[...] = jnp.zeros_like(run_ref)

    def logits(kb, hh):
        off = pl.multiple_of(kb * tile, tile)
        nz_ref[hh] = _dot_nt(q_ref[:, _hsl(hh)], k_ref[pl.ds(off, tile), _hsl(hh)])

    def block(kb, masked, kb_next):
        off = pl.multiple_of(kb * tile, tile)

        def stage_softplus(hh):
            nz = nz_ref[hh]
            neg_abs = lax.bitcast_convert_type(
                lax.bitcast_convert_type(nz, jnp.uint32) | jnp.uint32(0x80000000), F32)
            log_1m = jnp.minimum(nz, 0.0) - jnp.log2(1.0 + jnp.exp2(neg_abs))
            log_beta = log_1m - nz
            if masked:
                log_1m = jnp.where(causal, log_1m, 0.0)
            between = _dot(log_1m.astype(BF16), later)
            return log_beta, between, jnp.sum(log_1m, axis=-1, keepdims=True)

        def stage_out(hh, log_beta, between, total):
            att = jnp.exp2(log_beta + between)
            if masked:
                att = jnp.where(causal, att, 0.0)
            run = run_ref[hh]
            acc_ref[hh] += jnp.exp2(run) * _dot(att.astype(BF16), v_ref[pl.ds(off, tile), _hsl(hh)])
            run_ref[hh] = run + total

        mids = []
        for hh in heads:
            mids.append(stage_softplus(hh))
            logits(kb_next, hh)
        for hh in heads:
            stage_out(hh, *mids[hh])

    for hh in heads:
        logits(qi, hh)
    block(qi, True, jnp.maximum(qi - 1, 0))

    def body(it, carry):
        kb = qi - 1 - it
        block(kb, False, jnp.maximum(kb - 1, 0))
        return carry

    lax.fori_loop(0, qi, body, 0)
    for hh in heads:
        o_ref[:, _hsl(hh)] = acc_ref[hh].astype(o_ref.dtype)


def _sb_attention(qk, proj, w_mix):
    s = qk.shape[0]
    h = w_mix // HEAD_DIM
    hps = min(ATT_HEADS_PER_STEP, h)
    hg = h // hps
    wide = hps * HEAD_DIM
    tile = min(ATT_TILE, s)
    return pl.pallas_call(
        functools.partial(_sb_kernel, tile=tile, hps=hps),
        grid=(hg, s // tile),
        in_specs=[pl.BlockSpec((tile, wide), lambda g, i: (i, g)),
                  pl.BlockSpec((s, wide), lambda g, i: (0, hg + g)),
                  pl.BlockSpec((s, wide), lambda g, i: (0, 2 * hg + g))],
        out_specs=pl.BlockSpec((tile, wide), lambda g, i: (i, g)),
        out_shape=jax.ShapeDtypeStruct((s, w_mix), BF16),
        scratch_shapes=[pltpu.VMEM((hps, tile, HEAD_DIM), F32),
                        pltpu.VMEM((hps, tile, 1), F32),
                        pltpu.VMEM((hps, tile, tile), F32)],
        compiler_params=_params("parallel", "arbitrary"),
        name="sb_attention",
    )(qk, qk, proj)


def _hgrn2_kernel(f_ref, v_ref, q_ref, g_ref, lbl_ref, gain_ref, o_ref, st_ref, *, layer, rows, hps):
    @pl.when(pl.program_id(1) == 0)
    def _():
        st_ref[...] = jnp.zeros_like(st_ref)

    logits = lbl_ref[...]
    e = jnp.exp(logits - jnp.max(logits, axis=0, keepdims=True))
    p = e / jnp.sum(e, axis=0, keepdims=True)
    lb_all = jnp.zeros((1, hps * HEAD_DIM), F32)
    for i in range(1, layer + 1):
        lb_all = lb_all + p[i:i + 1, :]

    L = REC_CHUNK
    r_i = lax.broadcasted_iota(jnp.int32, (L, L), 0)
    c_i = lax.broadcasted_iota(jnp.int32, (L, L), 1)
    tri_incl = (c_i <= r_i).astype(BF16)
    half_row = lax.broadcasted_iota(jnp.int32, (SUB // 2, HEAD_DIM), 0)

    def chunk(c, carry):
        rs = pl.ds(pl.multiple_of(c * L, L), L)
        for hh in range(hps):
            lb = lb_all[:, _hsl(hh)]
            log_lb = jnp.log(jnp.maximum(lb, LB_FLOOR))
            x = f_ref[rs, _hsl(hh)]
            b = jnp.log1p(-lb) + _log_sigmoid(x)
            lf = jnp.maximum(log_lb, b) + jnp.log(1.0 + jnp.exp(-jnp.abs(log_lb - b)))
            kk = (1.0 - lb) * jax.nn.sigmoid(-x)
            qq = _silu(q_ref[rs, _hsl(hh)].astype(F32))
            v = v_ref[rs, _hsl(hh)]
            vf = v.astype(F32)

            hi, mid, lo = _split3(lf)
            cum = (_dot(tri_incl, hi) + _dot(tri_incl, mid) + _dot(tri_incl, lo)) * LOG2_E
            last = cum[L - 1:L, :]

            st = st_ref[hh]
            out = _dot_nt((qq * jnp.exp2(cum)).astype(BF16), st.astype(BF16))

            a_rows = [jnp.zeros((SUB, L), F32)]
            for i in range(1, L // SUB):
                lo_r, hi_r = i * SUB, (i + 1) * SUB
                anchor = cum[lo_r - 1:lo_r, :]
                qt = qq[lo_r:hi_r, :] * jnp.exp2(cum[lo_r:hi_r, :] - anchor)
                kt = kk[:lo_r, :] * jnp.exp2(anchor - cum[:lo_r, :])
                kt = jnp.concatenate([kt, jnp.zeros((L - lo_r, HEAD_DIM), F32)], axis=0)
                a_rows.append(_dot_nt(qt.astype(BF16), kt.astype(BF16)))
            a_off = jnp.concatenate(a_rows, axis=0)
            out = out + _dot(a_off.astype(BF16), v)

            diag = []
            for i in range(L // SUB):
                base = i * SUB
                halves = []
                for r0 in (0, SUB // 2):
                    rows_r = slice(base + r0, base + r0 + SUB // 2)
                    cb, qb = cum[rows_r, :], qq[rows_r, :]
                    o_half = jnp.zeros((SUB // 2, HEAD_DIM), F32)
                    for s_ in range(r0 + SUB // 2):
                        srow = slice(base + s_, base + s_ + 1)
                        diff = cb - cum[srow, :]
                        if s_ > r0:
                            diff = jnp.where(half_row >= s_ - r0, diff, NEG_BIG)
                        a_col = jnp.sum(qb * jnp.exp2(diff) * kk[srow, :], axis=-1, keepdims=True)
                        o_half = o_half + a_col * vf[srow, :]
                    halves.append(o_half)
                diag.extend(halves)
            out = out + jnp.concatenate(diag, axis=0)

            kt = kk * jnp.exp2(last - cum)
            st_ref[hh] = jnp.exp2(last) * st + _dot(vf.T.astype(BF16), kt.astype(BF16))

            ms = jnp.mean(out * out, axis=-1, keepdims=True)
            y = out * lax.rsqrt(ms + EPS) * gain_ref[...]
            o_ref[rs, _hsl(hh)] = (y * _silu(g_ref[rs, _hsl(hh)].astype(F32))).astype(o_ref.dtype)
        return carry

    lax.fori_loop(0, rows // L, chunk, 0)


def _hgrn2(f_pre, proj, lb_logits, out_gain, w_mix, layer):
    s = f_pre.shape[0]
    h = w_mix // HEAD_DIM
    hps = min(HGRN_HEADS_PER_STEP, h)
    hg = h // hps
    wide = hps * HEAD_DIM
    rows = min(REC_ROWS, s)
    depth = lb_logits.shape[0]
    blk = lambda grp: pl.BlockSpec((rows, wide), lambda g, c: (c, grp * hg + g))
    return pl.pallas_call(
        functools.partial(_hgrn2_kernel, layer=layer, rows=rows, hps=hps),
        grid=(hg, s // rows),
        in_specs=[pl.BlockSpec((rows, wide), lambda g, c: (c, g)),
                  blk(3), blk(4), blk(5),
                  pl.BlockSpec((depth, wide), lambda g, c: (0, g)),
                  pl.BlockSpec((None, 1, HEAD_DIM), lambda g, c: (layer, 0, 0))],
        out_specs=pl.BlockSpec((rows, wide), lambda g, c: (c, g)),
        out_shape=jax.ShapeDtypeStruct((s, w_mix), BF16),
        scratch_shapes=[pltpu.VMEM((hps, HEAD_DIM, HEAD_DIM), F32)],
        compiler_params=_params("parallel", "arbitrary"),
        name="hgrn2",
    )(f_pre, proj, proj, proj, lb_logits, out_gain.reshape(out_gain.shape[0], 1, HEAD_DIM))


def _mlstm_pre_kernel(x_ref, prev_ref, v_ref, cw_ref, cb_ref, wq_ref, wk_ref, wif_ref, bif_ref,
                      xc_ref, q_ref, k_ref, gates_ref, *, n_heads, conv_k):
    i = pl.program_id(0)
    x = x_ref[...].astype(F32)
    prev = prev_ref[8:16, :].astype(F32)
    prev = jnp.where(i > 0, prev, 0.0)
    row8 = lax.broadcasted_iota(jnp.int32, prev.shape, 0)
    acc = x * cw_ref[conv_k - 1:conv_k, :] + cb_ref[...]
    for back in range(1, conv_k):
        sh = pltpu.roll(x, back, 0)
        head = jnp.where(row8 < back, pltpu.roll(prev, back, 0), sh[:8, :])
        sh = jnp.concatenate([head, sh[8:, :]], axis=0)
        acc = acc + sh * cw_ref[conv_k - 1 - back:conv_k - back, :]
    xc = _silu(acc)
    xc_ref[...] = xc.astype(xc_ref.dtype)
    xcb = xc.astype(BF16)
    scale = HEAD_DIM ** -0.5
    w_mix = n_heads * HEAD_DIM
    g = jnp.zeros(gates_ref.shape, F32) + bif_ref[...]
    for h in range(n_heads):
        qh = _dot(xcb[:, _hsl(h)], wq_ref[h])
        kh = _dot(xcb[:, _hsl(h)], wk_ref[h])
        q_ref[:, _hsl(h)] = qh.astype(q_ref.dtype)
        k_ref[:, _hsl(h)] = (kh * scale).astype(k_ref.dtype)
        g = g + _dot_nt(wif_ref[:, _hsl(h)], qh.astype(BF16))
        g = g + _dot_nt(wif_ref[:, w_mix + h * HEAD_DIM:w_mix + (h + 1) * HEAD_DIM], kh.astype(BF16))
    g = g + _dot_nt(wif_ref[:, 2 * w_mix:], v_ref[...])
    gates_ref[...] = g


def _mlstm_pre(proj, conv_w, conv_b, w_q, w_k, w_if_t, b_if, layer, w_mix):
    s = proj.shape[0]
    h = w_mix // HEAD_DIM
    bm = min(512, s)
    conv_k = conv_w.shape[1]
    nl = conv_w.shape[0]
    lay3 = lambda i: (layer, 0, 0)
    return pl.pallas_call(
        functools.partial(_mlstm_pre_kernel, n_heads=h, conv_k=conv_k),
        grid=(s // bm,),
        in_specs=[pl.BlockSpec((bm, w_mix), lambda i: (i, 6)),
                  pl.BlockSpec((16, w_mix), lambda i: (jnp.maximum(i * (bm // 16) - 1, 0), 6)),
                  pl.BlockSpec((bm, w_mix), lambda i: (i, 7)),
                  pl.BlockSpec((None, conv_k, w_mix), lay3),
                  pl.BlockSpec((None, 1, w_mix), lay3),
                  pl.BlockSpec((None, h, HEAD_DIM, HEAD_DIM), lambda i: (layer, 0, 0, 0)),
                  pl.BlockSpec((None, h, HEAD_DIM, HEAD_DIM), lambda i: (layer, 0, 0, 0)),
                  pl.BlockSpec((None, 2 * h, 3 * w_mix), lay3),
                  pl.BlockSpec((None, 2 * h, 1), lay3)],
        out_specs=[pl.BlockSpec((bm, w_mix), lambda i: (i, 0)),
                   pl.BlockSpec((bm, w_mix), lambda i: (i, 0)),
                   pl.BlockSpec((bm, w_mix), lambda i: (i, 0)),
                   pl.BlockSpec((2 * h, bm), lambda i: (0, i))],
        out_shape=[jax.ShapeDtypeStruct((s, w_mix), BF16),
                   jax.ShapeDtypeStruct((s, w_mix), BF16),
                   jax.ShapeDtypeStruct((s, w_mix), BF16),
                   jax.ShapeDtypeStruct((2 * h, s), F32)],
        compiler_params=_params("parallel"),
        name="mlstm_pre",
    )(proj, proj, proj, conv_w, conv_b.reshape(nl, 1, w_mix), w_q, w_k, w_if_t, b_if.reshape(nl, 2 * h, 1))


def _mlstm_kernel(q_ref, k_ref, v_ref, li_ref, gf_ref, xc_ref, z_ref, skip_ref, gain_ref,
                  o_ref, c_ref, n_ref, m_ref, *, rows, hps):
    @pl.when(pl.program_id(1) == 0)
    def _():
        c_ref[...] = jnp.zeros_like(c_ref)
        n_ref[...] = jnp.zeros_like(n_ref)
        m_ref[...] = jnp.zeros_like(m_ref)

    L = REC_CHUNK
    r_i = lax.broadcasted_iota(jnp.int32, (L, L), 0)
    c_i = lax.broadcasted_iota(jnp.int32, (L, L), 1)
    upto = (r_i <= c_i).astype(BF16)
    causal = c_i <= r_i

    def chunk(c, carry):
        rs = pl.ds(pl.multiple_of(c * L, L), L)
        heads = range(hps)

        def stage_gates(hh):
            li = li_ref[0, hh, :, rs]
            lf = _log_sigmoid(gf_ref[0, hh, :, rs])
            hi, mid, lo = _split3(jnp.broadcast_to(lf, (8, L)))
            cum = (_dot(hi, upto) + _dot(mid, upto) + _dot(lo, upto))[0:1, :]
            return li, cum

        def stage_weights(hh, li, cum):
            q = q_ref[rs, _hsl(hh)]
            k = k_ref[rs, _hsl(hh)]
            cum_t = jnp.broadcast_to(cum, (L, L)).T
            m_old = m_ref[hh]
            dmat = jnp.where(causal, cum_t - cum + li, NEG_BIG)
            inter = cum_t[:, 0:1] + m_old
            m_t = jnp.maximum(inter, jnp.max(dmat, axis=-1, keepdims=True))
            w = jnp.exp(dmat - m_t) * _dot_nt(q, k)
            return w, jnp.exp(inter - m_t), m_t

        def stage_output(hh, w, carry_w, m_t):
            q = q_ref[rs, _hsl(hh)]
            v = v_ref[rs, _hsl(hh)]
            num = _dot(w.astype(BF16), v) + carry_w * _dot(q, c_ref[hh].astype(BF16))
            den = (jnp.sum(w, axis=-1, keepdims=True)
                   + carry_w * jnp.sum(q.astype(F32) * n_ref[hh], axis=-1, keepdims=True))
            h_out = num / jnp.maximum(jnp.abs(den), jnp.exp(-m_t))
            ms = jnp.mean(h_out * h_out, axis=-1, keepdims=True)
            hn = h_out * lax.rsqrt(ms + EPS) * gain_ref[...]
            y = ((hn + skip_ref[:, _hsl(hh)] * xc_ref[rs, _hsl(hh)].astype(F32))
                 * _silu(z_ref[rs, _hsl(hh)].astype(F32)))
            o_ref[rs, _hsl(hh)] = y.astype(o_ref.dtype)

        def stage_state(hh, li, cum):
            k = k_ref[rs, _hsl(hh)]
            v = v_ref[rs, _hsl(hh)]
            m_old = m_ref[hh]
            last = cum[:, L - 1:L]
            g_row = last - cum + li
            m_new = jnp.maximum(last + m_old, jnp.max(g_row, axis=-1, keepdims=True))
            wk = jnp.exp(g_row - m_new)
            kw = k.astype(F32) * jnp.broadcast_to(wk, (L, L)).T
            decay = jnp.exp(last + m_old - m_new)
            c_ref[hh] = decay * c_ref[hh] + _dot(kw.T.astype(BF16), v)
            n_ref[hh] = decay * n_ref[hh] + jnp.sum(kw, axis=0, keepdims=True)
            m_ref[hh] = m_new

        gates = [stage_gates(hh) for hh in heads]
        weights = [stage_weights(hh, *gates[hh]) for hh in heads]
        for hh in heads:
            stage_output(hh, *weights[hh])
        for hh in heads:
            stage_state(hh, *gates[hh])
        return carry

    lax.fori_loop(0, rows // L, chunk, 0)


def _mlstm(q_c, k_c, proj, gates, xc, skip, out_gain, layer, w_mix):
    s = q_c.shape[0]
    h = w_mix // HEAD_DIM
    hps = min(MLSTM_HEADS_PER_STEP, h)
    hg = h // hps
    wide = hps * HEAD_DIM
    rows = min(REC_ROWS, s)
    nl = skip.shape[0]
    gates4 = gates.reshape(2, h, 1, s)
    hd = lambda: pl.BlockSpec((rows, wide), lambda g, c: (c, g))
    return pl.pallas_call(
        functools.partial(_mlstm_kernel, rows=rows, hps=hps),
        grid=(hg, s // rows),
        in_specs=[hd(), hd(),
                  pl.BlockSpec((rows, wide), lambda g, c: (c, 7 * hg + g)),
                  pl.BlockSpec((1, hps, 1, rows), lambda g, c: (0, g, 0, c)),
                  pl.BlockSpec((1, hps, 1, rows), lambda g, c: (1, g, 0, c)),
                  hd(),
                  pl.BlockSpec((rows, wide), lambda g, c: (c, 8 * hg + g)),
                  pl.BlockSpec((None, 1, wide), lambda g, c: (layer, 0, g)),
                  pl.BlockSpec((None, 1, HEAD_DIM), lambda g, c: (layer, 0, 0))],
        out_specs=hd(),
        out_shape=jax.ShapeDtypeStruct((s, w_mix), BF16),
        scratch_shapes=[pltpu.VMEM((hps, HEAD_DIM, HEAD_DIM), F32),
                        pltpu.VMEM((hps, 1, HEAD_DIM), F32),
                        pltpu.VMEM((hps, 1, 1), F32)],
        compiler_params=_params("parallel", "arbitrary"),
        name="mlstm",
    )(q_c, k_c, proj, gates4, gates4, xc, proj, skip.reshape(nl, 1, w_mix),
      out_gain.reshape(nl, 1, HEAD_DIM))


def _merge_kernel(xg_ref, ssq_ref, wg_ref, ya_ref, yb_ref, yc_ref, wb_ref, o_ref):
    xg = xg_ref[...]
    r = _row_factor(ssq_ref, xg.shape[1])
    out = None
    for b, y_ref in enumerate((ya_ref, yb_ref, yc_ref)):
        gate = jax.nn.sigmoid(r * _dot(xg, wg_ref[b]))
        term = gate * _dot(y_ref[...], wb_ref[b])
        out = term if out is None else out + term
    o_ref[...] = out.astype(o_ref.dtype)


def _merge(xg, ssq, w_gate, ys, w_branch, layer):
    s, d = xg.shape
    w_mix = ys[0].shape[1]
    bm, bn = min(ROW_TILE, s), min(MERGE_COL_TILE, d)
    yspec = pl.BlockSpec((bm, w_mix), lambda i, j: (i, 0))
    return pl.pallas_call(
        _merge_kernel,
        grid=(s // bm, d // bn),
        in_specs=[pl.BlockSpec((bm, d), lambda i, j: (i, 0)),
                  pl.BlockSpec((bm, SSQ_LANES), lambda i, j: (i, 0)),
                  pl.BlockSpec((None, 3, d, bn), lambda i, j: (layer, 0, 0, j)),
                  yspec, yspec, yspec,
                  pl.BlockSpec((3, None, w_mix, bn), lambda i, j: (0, layer, 0, j))],
        out_specs=pl.BlockSpec((bm, bn), lambda i, j: (i, j)),
        out_shape=jax.ShapeDtypeStruct((s, d), BF16),
        compiler_params=_params("parallel", "arbitrary"),
        name="merge",
    )(xg, ssq, w_gate, *ys, w_branch)


def kernel(x, ffn1_norm, ffn1_w_gate, ffn1_w_up, ffn1_w_down, mix_norm, w_in, sb_q_gain, sb_k_gain,
           hg_lb_logits, hg_out_gain, ml_conv_w, ml_conv_b, ml_w_q, ml_w_k, ml_w_if, ml_b_if,
           ml_out_gain, ml_skip, w_merge_gate, w_branch_a, w_branch_b, w_branch_c, w_out,
           ffn2_norm, ffn2_w_gate, ffn2_w_up, ffn2_w_down):
    batch, seq, d_model = x.shape
    depth = w_in.shape[0]
    w_mix = w_branch_a.shape[1]
    scale = HEAD_DIM ** -0.5
    bf = lambda a: a.astype(BF16)

    f1g, f1u, f1d = bf(ffn1_w_gate), bf(ffn1_w_up), bf(ffn1_w_down)
    f2g, f2u, f2d = bf(ffn2_w_gate), bf(ffn2_w_up), bf(ffn2_w_down)
    w_in_b, w_gate_b, w_out_b = bf(w_in), bf(w_merge_gate), bf(w_out)
    w_branch_b3 = jnp.stack([bf(w_branch_a), bf(w_branch_b), bf(w_branch_c)])
    wq_b, wk_b, wif_t = bf(ml_w_q), bf(ml_w_k), bf(jnp.swapaxes(ml_w_if, 1, 2))
    qk_gains = jnp.stack([sb_q_gain * (-scale * LOG2_E), sb_k_gain], axis=1).reshape(depth, 2, 1, HEAD_DIM)

    outs = []
    for bi in range(batch):
        xs = x[bi]
        xg, ssq = _norm_prep(xs, ffn1_norm, 0)
        for l in range(depth):
            hid = _ffn_up(xg, ssq, f1g, f1u, l)
            xs, xg, ssq = _mm_residual(hid, f1d, l, xs, 0.5, mix_norm, l)

            proj, f_pre = _in_proj(xg, ssq, w_in_b, l, w_mix)
            qk = _qk_norm(proj, qk_gains, l, w_mix)
            y_a = _sb_attention(qk, proj, w_mix)
            y_b = _hgrn2(f_pre, proj, hg_lb_logits, hg_out_gain, w_mix, l)
            xc, q_c, k_c, gates = _mlstm_pre(proj, ml_conv_w, ml_conv_b, wq_b, wk_b, wif_t, ml_b_if, l, w_mix)
            y_c = _mlstm(q_c, k_c, proj, gates, xc, ml_skip, ml_out_gain, l, w_mix)
            merged = _merge(xg, ssq, w_gate_b, (y_a, y_b, y_c), w_branch_b3, l)
            xs, xg, ssq = _mm_residual(merged, w_out_b, l, xs, 1.0, ffn2_norm, l)

            hid = _ffn_up(xg, ssq, f2g, f2u, l)
            if l + 1 < depth:
                xs, xg, ssq = _mm_residual(hid, f2d, l, xs, 0.5, ffn1_norm, l + 1)
            else:
                xs = _mm_residual(hid, f2d, l, xs, 0.5)
        outs.append(xs)
    return outs[0][None] if batch == 1 else jnp.stack(outs)
```

```python
import functools

import jax
import jax.numpy as jnp
from jax import lax
from jax.experimental import pallas as pl
from jax.experimental.pallas import tpu as pltpu

F32 = jnp.float32
BF16 = jnp.bfloat16

HEAD_DIM = 128
EPS = 1e-6
NEG_BIG = -1e30
LB_FLOOR = 1e-30
LOG2_E = 1.4426950408889634
STICK_FLOOR_LOG2 = -160.0

V7X_VMEM_LIMIT_BYTES = 56 * 1024 * 1024
ROW_TILE = 1024
COL_TILE = 512
PROJ_COL_TILE = 1024
MERGE_COL_TILE = 256
SSQ_LANES = 128
ATT_TILE = 256
ATT_SLAB = 16
REC_CHUNK = 128
REC_ROWS = 512
SUB = 16
HGRN_HEADS_PER_STEP = 2
MLSTM_HEADS_PER_STEP = 4
ATT_HEADS_PER_STEP = 4


def _params(*sem):
    return pltpu.CompilerParams(dimension_semantics=sem, vmem_limit_bytes=V7X_VMEM_LIMIT_BYTES)


def _dot(a, b):
    return jnp.dot(a, b, preferred_element_type=F32)


def _dot_nt(a, b):
    return lax.dot_general(a, b, (((1,), (1,)), ((), ())), preferred_element_type=F32)


def _split3(x):
    hi = x.astype(BF16)
    r1 = x - hi.astype(F32)
    mid = r1.astype(BF16)
    lo = (r1 - mid.astype(F32)).astype(BF16)
    return hi, mid, lo


def _softplus(z):
    return jnp.maximum(z, 0.0) + jnp.log(1.0 + jnp.exp(-jnp.abs(z)))


def _log_sigmoid(x):
    return jnp.minimum(x, 0.0) - jnp.log(1.0 + jnp.exp(-jnp.abs(x)))


def _silu(x):
    return x * jax.nn.sigmoid(x)


def _hsl(hh):
    return slice(hh * HEAD_DIM, (hh + 1) * HEAD_DIM)


def _row_factor(ssq_ref, d):
    return lax.rsqrt(ssq_ref[:, 0:1] * (1.0 / d) + EPS)


def _norm_prep_kernel(x_ref, g_ref, xg_ref, ssq_ref):
    x = x_ref[...]
    xg_ref[...] = (x * g_ref[...]).astype(xg_ref.dtype)
    ssq_ref[...] = jnp.broadcast_to(jnp.sum(x * x, axis=-1, keepdims=True), ssq_ref.shape)


def _norm_prep(x, gains, layer):
    s, d = x.shape
    bm = min(256, s)
    return pl.pallas_call(
        _norm_prep_kernel,
        grid=(s // bm,),
        in_specs=[pl.BlockSpec((bm, d), lambda i: (i, 0)),
                  pl.BlockSpec((None, 1, d), lambda i: (layer, 0, 0))],
        out_specs=[pl.BlockSpec((bm, d), lambda i: (i, 0)),
                   pl.BlockSpec((bm, SSQ_LANES), lambda i: (i, 0))],
        out_shape=[jax.ShapeDtypeStruct((s, d), BF16),
                   jax.ShapeDtypeStruct((s, SSQ_LANES), F32)],
        compiler_params=_params("parallel"),
        name="norm_prep",
    )(x, gains.reshape(gains.shape[0], 1, d))


def _ffn_up_kernel(xg_ref, ssq_ref, wg_ref, wu_ref, o_ref):
    xg = xg_ref[...]
    r = _row_factor(ssq_ref, xg.shape[1])
    g = r * _dot(xg, wg_ref[...])
    u = r * _dot(xg, wu_ref[...])
    o_ref[...] = (_silu(g) * u).astype(o_ref.dtype)


def _ffn_up(xg, ssq, wg, wu, layer):
    s, d = xg.shape
    f = wg.shape[2]
    bm, bn = min(ROW_TILE, s), min(COL_TILE, f)
    wspec = pl.BlockSpec((None, d, bn), lambda i, j: (layer, 0, j))
    return pl.pallas_call(
        _ffn_up_kernel,
        grid=(s // bm, f // bn),
        in_specs=[pl.BlockSpec((bm, d), lambda i, j: (i, 0)),
                  pl.BlockSpec((bm, SSQ_LANES), lambda i, j: (i, 0)), wspec, wspec],
        out_specs=pl.BlockSpec((bm, bn), lambda i, j: (i, j)),
        out_shape=jax.ShapeDtypeStruct((s, f), BF16),
        compiler_params=_params("parallel", "arbitrary"),
        name="ffn_up",
    )(xg, ssq, wg, wu)


def _mm_res_kernel(a_ref, w_ref, x_ref, o_ref):
    o_ref[...] = x_ref[...] + _dot(a_ref[...], w_ref[...])


def _mm_res_norm_kernel(a_ref, w_ref, x_ref, g_ref, o_ref, xg_ref, ssq_ref):
    y = x_ref[...] + _dot(a_ref[...], w_ref[...])
    o_ref[...] = y
    xg_ref[...] = (y * g_ref[...]).astype(xg_ref.dtype)
    part = jnp.broadcast_to(jnp.sum(y * y, axis=-1, keepdims=True), ssq_ref.shape)

    @pl.when(pl.program_id(1) == 0)
    def _():
        ssq_ref[...] = part

    @pl.when(pl.program_id(1) > 0)
    def _():
        ssq_ref[...] += part


def _mm_residual(a, w, layer, x, next_gain=None, next_layer=0):
    s, k = a.shape
    n = w.shape[2]
    bm, bn = min(ROW_TILE, s), min(COL_TILE, n)
    in_specs = [pl.BlockSpec((bm, k), lambda i, j: (i, 0)),
                pl.BlockSpec((None, k, bn), lambda i, j: (layer, 0, j)),
                pl.BlockSpec((bm, bn), lambda i, j: (i, j))]
    tile = pl.BlockSpec((bm, bn), lambda i, j: (i, j))
    if next_gain is None:
        return pl.pallas_call(
            _mm_res_kernel,
            grid=(s // bm, n // bn),
            in_specs=in_specs,
            out_specs=tile,
            out_shape=jax.ShapeDtypeStruct((s, n), F32),
            compiler_params=_params("parallel", "arbitrary"),
            name="mm_residual",
        )(a, w, x)
    return pl.pallas_call(
        _mm_res_norm_kernel,
        grid=(s // bm, n // bn),
        in_specs=in_specs + [pl.BlockSpec((None, 1, bn), lambda i, j: (next_layer, 0, j))],
        out_specs=[tile, tile, pl.BlockSpec((bm, SSQ_LANES), lambda i, j: (i, 0))],
        out_shape=[jax.ShapeDtypeStruct((s, n), F32),
                   jax.ShapeDtypeStruct((s, n), BF16),
                   jax.ShapeDtypeStruct((s, SSQ_LANES), F32)],
        compiler_params=_params("parallel", "arbitrary"),
        name="mm_residual_norm",
    )(a, w, x, next_gain.reshape(next_gain.shape[0], 1, n))


def _mm_kernel(xg_ref, ssq_ref, w_ref, o_ref):
    xg = xg_ref[...]
    o_ref[...] = (_row_factor(ssq_ref, xg.shape[1]) * _dot(xg, w_ref[...])).astype(o_ref.dtype)


def _mm(xg, ssq, w, layer, n_out, bn, col_block_of, out_dtype):
    s, k = xg.shape
    bm = min(ROW_TILE, s)
    return pl.pallas_call(
        _mm_kernel,
        grid=(s // bm, n_out // bn),
        in_specs=[pl.BlockSpec((bm, k), lambda i, j: (i, 0)),
                  pl.BlockSpec((bm, SSQ_LANES), lambda i, j: (i, 0)),
                  pl.BlockSpec((None, k, bn), lambda i, j: (layer, 0, col_block_of(j)))],
        out_specs=pl.BlockSpec((bm, bn), lambda i, j: (i, j)),
        out_shape=jax.ShapeDtypeStruct((s, n_out), out_dtype),
        compiler_params=_params("parallel", "arbitrary"),
        name="in_proj",
    )(xg, ssq, w)


def _in_proj(xg, ssq, w_in, layer, w_mix):
    bn = min(PROJ_COL_TILE, w_mix)
    nblk = w_mix // bn
    proj = _mm(xg, ssq, w_in, layer, 9 * w_mix, bn, lambda j: j + jnp.where(j >= 3 * nblk, nblk, 0), BF16)
    f_pre = _mm(xg, ssq, w_in, layer, w_mix, bn, lambda j: j + 3 * nblk, F32)
    return proj, f_pre


def _qk_norm_kernel(x_ref, g_ref, o_ref, *, n_heads):
    g = g_ref[...]
    for h in range(n_heads):
        x = x_ref[:, _hsl(h)].astype(F32)
        ms = jnp.mean(x * x, axis=-1, keepdims=True)
        o_ref[:, _hsl(h)] = (x * lax.rsqrt(ms + EPS) * g).astype(o_ref.dtype)


def _qk_norm(proj, gains, layer, w_mix):
    s = proj.shape[0]
    bm = min(512, s)
    return pl.pallas_call(
        functools.partial(_qk_norm_kernel, n_heads=w_mix // HEAD_DIM),
        grid=(s // bm, 2),
        in_specs=[pl.BlockSpec((bm, w_mix), lambda i, g: (i, g)),
                  pl.BlockSpec((None, None, 1, HEAD_DIM), lambda i, g: (layer, g, 0, 0))],
        out_specs=pl.BlockSpec((bm, w_mix), lambda i, g: (i, g)),
        out_shape=jax.ShapeDtypeStruct((s, 2 * w_mix), BF16),
        compiler_params=_params("parallel", "arbitrary"),
        name="qk_norm",
    )(proj, gains)


def _sb_kernel(q_ref, k_ref, v_ref, o_ref, acc_ref, run_ref, nz_ref, *, tile, hps):
    qi = pl.program_id(1)
    row = lax.broadcasted_iota(jnp.int32, (tile, tile), 0)
    col = lax.broadcasted_iota(jnp.int32, (tile, tile), 1)
    later = (row > col).astype(BF16)
    causal = col < row
    heads = range(hps)

    acc_ref[...] = jnp.zeros_like(acc_ref)
    run_ref[...] = jnp.zeros_like(run_ref)

    def logits(kb, hh):
        off = pl.multiple_of(kb * tile, tile)
        nz_ref[hh] = _dot_nt(q_ref[:, _hsl(hh)], k_ref[pl.ds(off, tile), _hsl(hh)])

    def block(kb, masked, kb_next):
        off = pl.multiple_of(kb * tile, tile)

        def stage_softplus(hh):
            nz = nz_ref[hh]
            neg_abs = lax.bitcast_convert_type(
                lax.bitcast_convert_type(nz, jnp.uint32) | jnp.uint32(0x80000000), F32)
            log_1m = jnp.minimum(nz, 0.0) - jnp.log2(1.0 + jnp.exp2(neg_abs))
            log_beta = log_1m - nz
            if masked:
                log_1m = jnp.where(causal, log_1m, 0.0)
            between = _dot(log_1m.astype(BF16), later)
            return log_beta, between, jnp.sum(log_1m, axis=-1, keepdims=True)

        def stage_out(hh, log_beta, between, total):
            att = jnp.exp2(log_beta + between)
            if masked:
                att = jnp.where(causal, att, 0.0)
            run = run_ref[hh]
            acc_ref[hh] += jnp.exp2(run) * _dot(att.astype(BF16), v_ref[pl.ds(off, tile), _hsl(hh)])
            run_ref[hh] = run + total

        mids = []
        for hh in heads:
            mids.append(stage_softplus(hh))
            logits(kb_next, hh)
        for hh in heads:
            stage_out(hh, *mids[hh])

    for hh in heads:
        logits(qi, hh)
    block(qi, True, jnp.maximum(qi - 1, 0))

    def stick_left():
        top = run_ref[0]
        for hh in range(1, hps):
            top = jnp.maximum(top, run_ref[hh])
        return jnp.max(top) > STICK_FLOOR_LOG2

    def body(state):
        it, _ = state
        kb = qi - 1 - it
        block(kb, False, jnp.maximum(kb - 1, 0))
        return it + 1, stick_left()

    lax.while_loop(lambda st: jnp.logical_and(st[0] < qi, st[1]), body, (jnp.int32(0), stick_left()))
    for hh in heads:
        o_ref[:, _hsl(hh)] = acc_ref[hh].astype(o_ref.dtype)


def _sb_attention(qk, proj, w_mix):
    s = qk.shape[0]
    h = w_mix // HEAD_DIM
    hps = min(ATT_HEADS_PER_STEP, h)
    hg = h // hps
    wide = hps * HEAD_DIM
    tile = min(ATT_TILE, s)
    return pl.pallas_call(
        functools.partial(_sb_kernel, tile=tile, hps=hps),
        grid=(hg, s // tile),
        in_specs=[pl.BlockSpec((tile, wide), lambda g, i: (i, g)),
                  pl.BlockSpec((s, wide), lambda g, i: (0, hg + g)),
                  pl.BlockSpec((s, wide), lambda g, i: (0, 2 * hg + g))],
        out_specs=pl.BlockSpec((tile, wide), lambda g, i: (i, g)),
        out_shape=jax.ShapeDtypeStruct((s, w_mix), BF16),
        scratch_shapes=[pltpu.VMEM((hps, tile, HEAD_DIM), F32),
                        pltpu.VMEM((hps, tile, 1), F32),
                        pltpu.VMEM((hps, tile, tile), F32)],
        compiler_params=_params("parallel", "arbitrary"),
        name="sb_attention",
    )(qk, qk, proj)


def _hgrn2_kernel(f_ref, v_ref, q_ref, g_ref, lbl_ref, gain_ref, o_ref, st_ref, *, layer, rows, hps):
    @pl.when(pl.program_id(1) == 0)
    def _():
        st_ref[...] = jnp.zeros_like(st_ref)

    logits = lbl_ref[...]
    e = jnp.exp(logits - jnp.max(logits, axis=0, keepdims=True))
    p = e / jnp.sum(e, axis=0, keepdims=True)
    lb_all = jnp.zeros((1, hps * HEAD_DIM), F32)
    for i in range(1, layer + 1):
        lb_all = lb_all + p[i:i + 1, :]

    L = REC_CHUNK
    r_i = lax.broadcasted_iota(jnp.int32, (L, L), 0)
    c_i = lax.broadcasted_iota(jnp.int32, (L, L), 1)
    tri_incl = (c_i <= r_i).astype(BF16)
    half_row = lax.broadcasted_iota(jnp.int32, (SUB // 2, HEAD_DIM), 0)

    def chunk(c, carry):
        rs = pl.ds(pl.multiple_of(c * L, L), L)
        for hh in range(hps):
            lb = lb_all[:, _hsl(hh)]
            log_lb = jnp.log(jnp.maximum(lb, LB_FLOOR))
            x = f_ref[rs, _hsl(hh)]
            b = jnp.log1p(-lb) + _log_sigmoid(x)
            lf = jnp.maximum(log_lb, b) + jnp.log(1.0 + jnp.exp(-jnp.abs(log_lb - b)))
            kk = (1.0 - lb) * jax.nn.sigmoid(-x)
            qq = _silu(q_ref[rs, _hsl(hh)].astype(F32))
            v = v_ref[rs, _hsl(hh)]
            vf = v.astype(F32)

            hi, mid, lo = _split3(lf)
            cum = (_dot(tri_incl, hi) + _dot(tri_incl, mid) + _dot(tri_incl, lo)) * LOG2_E
            last = cum[L - 1:L, :]

            st = st_ref[hh]
            out = _dot_nt((qq * jnp.exp2(cum)).astype(BF16), st.astype(BF16))

            a_rows = [jnp.zeros((SUB, L), F32)]
            for i in range(1, L // SUB):
                lo_r, hi_r = i * SUB, (i + 1) * SUB
                anchor = cum[lo_r - 1:lo_r, :]
                qt = qq[lo_r:hi_r, :] * jnp.exp2(cum[lo_r:hi_r, :] - anchor)
                kt = kk[:lo_r, :] * jnp.exp2(anchor - cum[:lo_r, :])
                kt = jnp.concatenate([kt, jnp.zeros((L - lo_r, HEAD_DIM), F32)], axis=0)
                a_rows.append(_dot_nt(qt.astype(BF16), kt.astype(BF16)))
            a_off = jnp.concatenate(a_rows, axis=0)
            out = out + _dot(a_off.astype(BF16), v)

            diag = []
            for i in range(L // SUB):
                base = i * SUB
                halves = []
                for r0 in (0, SUB // 2):
                    rows_r = slice(base + r0, base + r0 + SUB // 2)
                    cb, qb = cum[rows_r, :], qq[rows_r, :]
                    o_half = jnp.zeros((SUB // 2, HEAD_DIM), F32)
                    for s_ in range(r0 + SUB // 2):
                        srow = slice(base + s_, base + s_ + 1)
                        diff = cb - cum[srow, :]
                        if s_ > r0:
                            diff = jnp.where(half_row >= s_ - r0, diff, NEG_BIG)
                        a_col = jnp.sum(qb * jnp.exp2(diff) * kk[srow, :], axis=-1, keepdims=True)
                        o_half = o_half + a_col * vf[srow, :]
                    halves.append(o_half)
                diag.extend(halves)
            out = out + jnp.concatenate(diag, axis=0)

            kt = kk * jnp.exp2(last - cum)
            st_ref[hh] = jnp.exp2(last) * st + _dot(vf.T.astype(BF16), kt.astype(BF16))

            ms = jnp.mean(out * out, axis=-1, keepdims=True)
            y = out * lax.rsqrt(ms + EPS) * gain_ref[...]
            o_ref[rs, _hsl(hh)] = (y * _silu(g_ref[rs, _hsl(hh)].astype(F32))).astype(o_ref.dtype)
        return carry

    lax.fori_loop(0, rows // L, chunk, 0)


def _hgrn2(f_pre, proj, lb_logits, out_gain, w_mix, layer):
    s = f_pre.shape[0]
    h = w_mix // HEAD_DIM
    hps = min(HGRN_HEADS_PER_STEP, h)
    hg = h // hps
    wide = hps * HEAD_DIM
    rows = min(REC_ROWS, s)
    depth = lb_logits.shape[0]
    blk = lambda grp: pl.BlockSpec((rows, wide), lambda g, c: (c, grp * hg + g))
    return pl.pallas_call(
        functools.partial(_hgrn2_kernel, layer=layer, rows=rows, hps=hps),
        grid=(hg, s // rows),
        in_specs=[pl.BlockSpec((rows, wide), lambda g, c: (c, g)),
                  blk(3), blk(4), blk(5),
                  pl.BlockSpec((depth, wide), lambda g, c: (0, g)),
                  pl.BlockSpec((None, 1, HEAD_DIM), lambda g, c: (layer, 0, 0))],
        out_specs=pl.BlockSpec((rows, wide), lambda g, c: (c, g)),
        out_shape=jax.ShapeDtypeStruct((s, w_mix), BF16),
        scratch_shapes=[pltpu.VMEM((hps, HEAD_DIM, HEAD_DIM), F32)],
        compiler_params=_params("parallel", "arbitrary"),
        name="hgrn2",
    )(f_pre, proj, proj, proj, lb_logits, out_gain.reshape(out_gain.shape[0], 1, HEAD_DIM))


def _mlstm_pre_kernel(x_ref, prev_ref, v_ref, cw_ref, cb_ref, wq_ref, wk_ref, wif_ref, bif_ref,
                      xc_ref, q_ref, k_ref, gates_ref, *, n_heads, conv_k):
    i = pl.program_id(0)
    x = x_ref[...].astype(F32)
    prev = prev_ref[8:16, :].astype(F32)
    prev = jnp.where(i > 0, prev, 0.0)
    row8 = lax.broadcasted_iota(jnp.int32, prev.shape, 0)
    acc = x * cw_ref[conv_k - 1:conv_k, :] + cb_ref[...]
    for back in range(1, conv_k):
        sh = pltpu.roll(x, back, 0)
        head = jnp.where(row8 < back, pltpu.roll(prev, back, 0), sh[:8, :])
        sh = jnp.concatenate([head, sh[8:, :]], axis=0)
        acc = acc + sh * cw_ref[conv_k - 1 - back:conv_k - back, :]
    xc = _silu(acc)
    xc_ref[...] = xc.astype(xc_ref.dtype)
    xcb = xc.astype(BF16)
    scale = HEAD_DIM ** -0.5
    w_mix = n_heads * HEAD_DIM
    g = jnp.zeros(gates_ref.shape, F32) + bif_ref[...]
    for h in range(n_heads):
        qh = _dot(xcb[:, _hsl(h)], wq_ref[h])
        kh = _dot(xcb[:, _hsl(h)], wk_ref[h])
        q_ref[:, _hsl(h)] = qh.astype(q_ref.dtype)
        k_ref[:, _hsl(h)] = (kh * scale).astype(k_ref.dtype)
        g = g + _dot_nt(wif_ref[:, _hsl(h)], qh.astype(BF16))
        g = g + _dot_nt(wif_ref[:, w_mix + h * HEAD_DIM:w_mix + (h + 1) * HEAD_DIM], kh.astype(BF16))
    g = g + _dot_nt(wif_ref[:, 2 * w_mix:], v_ref[...])
    gates_ref[...] = g


def _mlstm_pre(proj, conv_w, conv_b, w_q, w_k, w_if_t, b_if, layer, w_mix):
    s = proj.shape[0]
    h = w_mix // HEAD_DIM
    bm = min(512, s)
    conv_k = conv_w.shape[1]
    nl = conv_w.shape[0]
    lay3 = lambda i: (layer, 0, 0)
    return pl.pallas_call(
        functools.partial(_mlstm_pre_kernel, n_heads=h, conv_k=conv_k),
        grid=(s // bm,),
        in_specs=[pl.BlockSpec((bm, w_mix), lambda i: (i, 6)),
                  pl.BlockSpec((16, w_mix), lambda i: (jnp.maximum(i * (bm // 16) - 1, 0), 6)),
                  pl.BlockSpec((bm, w_mix), lambda i: (i, 7)),
                  pl.BlockSpec((None, conv_k, w_mix), lay3),
                  pl.BlockSpec((None, 1, w_mix), lay3),
                  pl.BlockSpec((None, h, HEAD_DIM, HEAD_DIM), lambda i: (layer, 0, 0, 0)),
                  pl.BlockSpec((None, h, HEAD_DIM, HEAD_DIM), lambda i: (layer, 0, 0, 0)),
                  pl.BlockSpec((None, 2 * h, 3 * w_mix), lay3),
                  pl.BlockSpec((None, 2 * h, 1), lay3)],
        out_specs=[pl.BlockSpec((bm, w_mix), lambda i: (i, 0)),
                   pl.BlockSpec((bm, w_mix), lambda i: (i, 0)),
                   pl.BlockSpec((bm, w_mix), lambda i: (i, 0)),
                   pl.BlockSpec((2 * h, bm), lambda i: (0, i))],
        out_shape=[jax.ShapeDtypeStruct((s, w_mix), BF16),
                   jax.ShapeDtypeStruct((s, w_mix), BF16),
                   jax.ShapeDtypeStruct((s, w_mix), BF16),
                   jax.ShapeDtypeStruct((2 * h, s), F32)],
        compiler_params=_params("parallel"),
        name="mlstm_pre",
    )(proj, proj, proj, conv_w, conv_b.reshape(nl, 1, w_mix), w_q, w_k, w_if_t, b_if.reshape(nl, 2 * h, 1))


def _mlstm_kernel(q_ref, k_ref, v_ref, li_ref, gf_ref, xc_ref, z_ref, skip_ref, gain_ref,
                  o_ref, c_ref, n_ref, m_ref, *, rows, hps):
    @pl.when(pl.program_id(1) == 0)
    def _():
        c_ref[...] = jnp.zeros_like(c_ref)
        n_ref[...] = jnp.zeros_like(n_ref)
        m_ref[...] = jnp.zeros_like(m_ref)

    L = REC_CHUNK
    r_i = lax.broadcasted_iota(jnp.int32, (L, L), 0)
    c_i = lax.broadcasted_iota(jnp.int32, (L, L), 1)
    upto = (r_i <= c_i).astype(BF16)
    causal = c_i <= r_i

    def chunk(c, carry):
        rs = pl.ds(pl.multiple_of(c * L, L), L)
        heads = range(hps)

        def stage_gates(hh):
            li = li_ref[0, hh, :, rs]
            lf = _log_sigmoid(gf_ref[0, hh, :, rs])
            hi, mid, lo = _split3(jnp.broadcast_to(lf, (8, L)))
            cum = (_dot(hi, upto) + _dot(mid, upto) + _dot(lo, upto))[0:1, :]
            return li, cum

        def stage_weights(hh, li, cum):
            q = q_ref[rs, _hsl(hh)]
            k = k_ref[rs, _hsl(hh)]
            cum_t = jnp.broadcast_to(cum, (L, L)).T
            m_old = m_ref[hh]
            dmat = jnp.where(causal, cum_t - cum + li, NEG_BIG)
            inter = cum_t[:, 0:1] + m_old
            m_t = jnp.maximum(inter, jnp.max(dmat, axis=-1, keepdims=True))
            w = jnp.exp(dmat - m_t) * _dot_nt(q, k)
            return w, jnp.exp(inter - m_t), m_t

        def stage_output(hh, w, carry_w, m_t):
            q = q_ref[rs, _hsl(hh)]
            v = v_ref[rs, _hsl(hh)]
            num = _dot(w.astype(BF16), v) + carry_w * _dot(q, c_ref[hh].astype(BF16))
            den = (jnp.sum(w, axis=-1, keepdims=True)
                   + carry_w * jnp.sum(q.astype(F32) * n_ref[hh], axis=-1, keepdims=True))
            h_out = num / jnp.maximum(jnp.abs(den), jnp.exp(-m_t))
            ms = jnp.mean(h_out * h_out, axis=-1, keepdims=True)
            hn = h_out * lax.rsqrt(ms + EPS) * gain_ref[...]
            y = ((hn + skip_ref[:, _hsl(hh)] * xc_ref[rs, _hsl(hh)].astype(F32))
                 * _silu(z_ref[rs, _hsl(hh)].astype(F32)))
            o_ref[rs, _hsl(hh)] = y.astype(o_ref.dtype)

        def stage_state(hh, li, cum):
            k = k_ref[rs, _hsl(hh)]
            v = v_ref[rs, _hsl(hh)]
            m_old = m_ref[hh]
            last = cum[:, L - 1:L]
            g_row = last - cum + li
            m_new = jnp.maximum(last + m_old, jnp.max(g_row, axis=-1, keepdims=True))
            wk = jnp.exp(g_row - m_new)
            kw = k.astype(F32) * jnp.broadcast_to(wk, (L, L)).T
            decay = jnp.exp(last + m_old - m_new)
            c_ref[hh] = decay * c_ref[hh] + _dot(kw.T.astype(BF16), v)
            n_ref[hh] = decay * n_ref[hh] + jnp.sum(kw, axis=0, keepdims=True)
            m_ref[hh] = m_new

        gates = [stage_gates(hh) for hh in heads]
        weights = [stage_weights(hh, *gates[hh]) for hh in heads]
        for hh in heads:
            stage_output(hh, *weights[hh])
        for hh in heads:
            stage_state(hh, *gates[hh])
        return carry

    lax.fori_loop(0, rows // L, chunk, 0)


def _mlstm(q_c, k_c, proj, gates, xc, skip, out_gain, layer, w_mix):
    s = q_c.shape[0]
    h = w_mix // HEAD_DIM
    hps = min(MLSTM_HEADS_PER_STEP, h)
    hg = h // hps
    wide = hps * HEAD_DIM
    rows = min(REC_ROWS, s)
    nl = skip.shape[0]
    gates4 = gates.reshape(2, h, 1, s)
    hd = lambda: pl.BlockSpec((rows, wide), lambda g, c: (c, g))
    return pl.pallas_call(
        functools.partial(_mlstm_kernel, rows=rows, hps=hps),
        grid=(hg, s // rows),
        in_specs=[hd(), hd(),
                  pl.BlockSpec((rows, wide), lambda g, c: (c, 7 * hg + g)),
                  pl.BlockSpec((1, hps, 1, rows), lambda g, c: (0, g, 0, c)),
                  pl.BlockSpec((1, hps, 1, rows), lambda g, c: (1, g, 0, c)),
                  hd(),
                  pl.BlockSpec((rows, wide), lambda g, c: (c, 8 * hg + g)),
                  pl.BlockSpec((None, 1, wide), lambda g, c: (layer, 0, g)),
                  pl.BlockSpec((None, 1, HEAD_DIM), lambda g, c: (layer, 0, 0))],
        out_specs=hd(),
        out_shape=jax.ShapeDtypeStruct((s, w_mix), BF16),
        scratch_shapes=[pltpu.VMEM((hps, HEAD_DIM, HEAD_DIM), F32),
                        pltpu.VMEM((hps, 1, HEAD_DIM), F32),
                        pltpu.VMEM((hps, 1, 1), F32)],
        compiler_params=_params("parallel", "arbitrary"),
        name="mlstm",
    )(q_c, k_c, proj, gates4, gates4, xc, proj, skip.reshape(nl, 1, w_mix),
      out_gain.reshape(nl, 1, HEAD_DIM))


def _merge_kernel(xg_ref, ssq_ref, wg_ref, ya_ref, yb_ref, yc_ref, wb_ref, o_ref):
    xg = xg_ref[...]
    r = _row_factor(ssq_ref, xg.shape[1])
    out = None
    for b, y_ref in enumerate((ya_ref, yb_ref, yc_ref)):
        gate = jax.nn.sigmoid(r * _dot(xg, wg_ref[b]))
        term = gate * _dot(y_ref[...], wb_ref[b])
        out = term if out is None else out + term
    o_ref[...] = out.astype(o_ref.dtype)


def _merge(xg, ssq, w_gate, ys, w_branch, layer):
    s, d = xg.shape
    w_mix = ys[0].shape[1]
    bm, bn = min(ROW_TILE, s), min(MERGE_COL_TILE, d)
    yspec = pl.BlockSpec((bm, w_mix), lambda i, j: (i, 0))
    return pl.pallas_call(
        _merge_kernel,
        grid=(s // bm, d // bn),
        in_specs=[pl.BlockSpec((bm, d), lambda i, j: (i, 0)),
                  pl.BlockSpec((bm, SSQ_LANES), lambda i, j: (i, 0)),
                  pl.BlockSpec((None, 3, d, bn), lambda i, j: (layer, 0, 0, j)),
                  yspec, yspec, yspec,
                  pl.BlockSpec((3, None, w_mix, bn), lambda i, j: (0, layer, 0, j))],
        out_specs=pl.BlockSpec((bm, bn), lambda i, j: (i, j)),
        out_shape=jax.ShapeDtypeStruct((s, d), BF16),
        compiler_params=_params("parallel", "arbitrary"),
        name="merge",
    )(xg, ssq, w_gate, *ys, w_branch)


def kernel(x, ffn1_norm, ffn1_w_gate, ffn1_w_up, ffn1_w_down, mix_norm, w_in, sb_q_gain, sb_k_gain,
           hg_lb_logits, hg_out_gain, ml_conv_w, ml_conv_b, ml_w_q, ml_w_k, ml_w_if, ml_b_if,
           ml_out_gain, ml_skip, w_merge_gate, w_branch_a, w_branch_b, w_branch_c, w_out,
           ffn2_norm, ffn2_w_gate, ffn2_w_up, ffn2_w_down):
    batch, seq, d_model = x.shape
    depth = w_in.shape[0]
    w_mix = w_branch_a.shape[1]
    scale = HEAD_DIM ** -0.5
    bf = lambda a: a.astype(BF16)

    f1g, f1u, f1d = bf(ffn1_w_gate), bf(ffn1_w_up), bf(0.5 * ffn1_w_down)
    f2g, f2u, f2d = bf(ffn2_w_gate), bf(ffn2_w_up), bf(0.5 * ffn2_w_down)
    w_in_b, w_gate_b, w_out_b = bf(w_in), bf(w_merge_gate), bf(w_out)
    w_branch_b3 = jnp.stack([bf(w_branch_a), bf(w_branch_b), bf(w_branch_c)])
    wq_b, wk_b, wif_t = bf(ml_w_q), bf(ml_w_k), bf(jnp.swapaxes(ml_w_if, 1, 2))
    qk_gains = jnp.stack([sb_q_gain * (-scale * LOG2_E), sb_k_gain], axis=1).reshape(depth, 2, 1, HEAD_DIM)

    outs = []
    for bi in range(batch):
        xs = x[bi]
        xg, ssq = _norm_prep(xs, ffn1_norm, 0)
        for l in range(depth):
            hid = _ffn_up(xg, ssq, f1g, f1u, l)
            xs, xg, ssq = _mm_residual(hid, f1d, l, xs, mix_norm, l)

            proj, f_pre = _in_proj(xg, ssq, w_in_b, l, w_mix)
            qk = _qk_norm(proj, qk_gains, l, w_mix)
            y_a = _sb_attention(qk, proj, w_mix)
            y_b = _hgrn2(f_pre, proj, hg_lb_logits, hg_out_gain, w_mix, l)
            xc, q_c, k_c, gates = _mlstm_pre(proj, ml_conv_w, ml_conv_b, wq_b, wk_b, wif_t, ml_b_if, l, w_mix)
            y_c = _mlstm(q_c, k_c, proj, gates, xc, ml_skip, ml_out_gain, l, w_mix)
            merged = _merge(xg, ssq, w_gate_b, (y_a, y_b, y_c), w_branch_b3, l)
            xs, xg, ssq = _mm_residual(merged, w_out_b, l, xs, ffn2_norm, l)

            hid = _ffn_up(xg, ssq, f2g, f2u, l)
            if l + 1 < depth:
                xs, xg, ssq = _mm_residual(hid, f2d, l, xs, ffn1_norm, l + 1)
            else:
                xs = _mm_residual(hid, f2d, l, xs)
        outs.append(xs)
    return outs[0][None] if batch == 1 else jnp.stack(outs)
```

```python
import functools

import jax
import jax.numpy as jnp
from jax import lax
from jax.experimental import pallas as pl
from jax.experimental.pallas import tpu as pltpu

F32 = jnp.float32
BF16 = jnp.bfloat16

HEAD_DIM = 128
EPS = 1e-6
NEG_BIG = -1e30
LB_FLOOR = 1e-30
LOG2_E = 1.4426950408889634
STICK_FLOOR_LOG2 = -160.0

V7X_VMEM_LIMIT_BYTES = 56 * 1024 * 1024
ROW_TILE = 1024
COL_TILE = 512
PROJ_COL_TILE = 1024
MERGE_COL_TILE = 256
SSQ_LANES = 128
ATT_TILE = 256
ATT_SLAB = 16
REC_CHUNK = 128
REC_ROWS = 512
SUB = 16
HGRN_HEADS_PER_STEP = 2
MLSTM_HEADS_PER_STEP = 4
ATT_HEADS_PER_STEP = 4


def _params(*sem):
    return pltpu.CompilerParams(dimension_semantics=sem, vmem_limit_bytes=V7X_VMEM_LIMIT_BYTES)


def _dot(a, b):
    return jnp.dot(a, b, preferred_element_type=F32)


def _dot_nt(a, b):
    return lax.dot_general(a, b, (((1,), (1,)), ((), ())), preferred_element_type=F32)


def _split3(x):
    hi = x.astype(BF16)
    r1 = x - hi.astype(F32)
    mid = r1.astype(BF16)
    lo = (r1 - mid.astype(F32)).astype(BF16)
    return hi, mid, lo


def _softplus(z):
    return jnp.maximum(z, 0.0) + jnp.log(1.0 + jnp.exp(-jnp.abs(z)))


def _log_sigmoid(x):
    return jnp.minimum(x, 0.0) - jnp.log(1.0 + jnp.exp(-jnp.abs(x)))


def _silu(x):
    return x * jax.nn.sigmoid(x)


def _hsl(hh):
    return slice(hh * HEAD_DIM, (hh + 1) * HEAD_DIM)


def _side_specs(sides, grid):
    steps = 1
    for n in grid:
        steps *= n

    def chunk(*g):
        idx = g[0]
        for a, n in zip(g[1:], grid[1:]):
            idx = idx * n + a
        return idx

    in_specs, out_specs, out_shapes = [], [], []
    for src, layer, _ in sides:
        _, r, c = src.shape
        rb = r // steps
        assert rb * steps == r and rb % 16 == 0, (src.shape, steps)
        in_specs.append(pl.BlockSpec((None, rb, c), lambda *g, layer=layer: (layer, chunk(*g), 0)))
        out_specs.append(pl.BlockSpec((rb, c), lambda *g: (chunk(*g), 0)))
        out_shapes.append(jax.ShapeDtypeStruct((r, c), BF16))
    return in_specs, out_specs, out_shapes


def _cast_sides(in_refs, out_refs, factors):
    for i_ref, o_ref, f in zip(in_refs, out_refs, factors):
        v = i_ref[...]
        o_ref[...] = (v if f == 1.0 else v * f).astype(o_ref.dtype)


def _split_refs(refs, n_in, n_out, n_side):
    a, b, c, d = n_in, n_in + n_side, n_in + n_side + n_out, n_in + 2 * n_side + n_out
    return refs[:a], refs[a:b], refs[b:c], refs[c:d], refs[d:]


def _row_factor(ssq_ref, d):
    return lax.rsqrt(ssq_ref[:, 0:1] * (1.0 / d) + EPS)


def _norm_prep_kernel(x_ref, g_ref, xg_ref, ssq_ref):
    x = x_ref[...]
    xg_ref[...] = (x * g_ref[...]).astype(xg_ref.dtype)
    ssq_ref[...] = jnp.broadcast_to(jnp.sum(x * x, axis=-1, keepdims=True), ssq_ref.shape)


def _norm_prep(x, gains, layer):
    s, d = x.shape
    bm = min(256, s)
    return pl.pallas_call(
        _norm_prep_kernel,
        grid=(s // bm,),
        in_specs=[pl.BlockSpec((bm, d), lambda i: (i, 0)),
                  pl.BlockSpec((None, 1, d), lambda i: (layer, 0, 0))],
        out_specs=[pl.BlockSpec((bm, d), lambda i: (i, 0)),
                   pl.BlockSpec((bm, SSQ_LANES), lambda i: (i, 0))],
        out_shape=[jax.ShapeDtypeStruct((s, d), BF16),
                   jax.ShapeDtypeStruct((s, SSQ_LANES), F32)],
        compiler_params=_params("parallel"),
        name="norm_prep",
    )(x, gains.reshape(gains.shape[0], 1, d))


def _ffn_up_kernel(xg_ref, ssq_ref, wg_ref, wu_ref, o_ref):
    xg = xg_ref[...]
    r = _row_factor(ssq_ref, xg.shape[1])
    g = r * _dot(xg, wg_ref[...])
    u = r * _dot(xg, wu_ref[...])
    o_ref[...] = (_silu(g) * u).astype(o_ref.dtype)


def _ffn_up(xg, ssq, wg, wu):
    s, d = xg.shape
    f = wg.shape[1]
    bm, bn = min(ROW_TILE, s), min(COL_TILE, f)
    wspec = pl.BlockSpec((d, bn), lambda i, j: (0, j))
    return pl.pallas_call(
        _ffn_up_kernel,
        grid=(s // bm, f // bn),
        in_specs=[pl.BlockSpec((bm, d), lambda i, j: (i, 0)),
                  pl.BlockSpec((bm, SSQ_LANES), lambda i, j: (i, 0)), wspec, wspec],
        out_specs=pl.BlockSpec((bm, bn), lambda i, j: (i, j)),
        out_shape=jax.ShapeDtypeStruct((s, f), BF16),
        compiler_params=_params("parallel", "arbitrary"),
        name="ffn_up",
    )(xg, ssq, wg, wu)


def _mm_res_kernel(a_ref, w_ref, x_ref, o_ref):
    o_ref[...] = x_ref[...] + _dot(a_ref[...], w_ref[...])


def _mm_res_norm_kernel(a_ref, w_ref, x_ref, g_ref, o_ref, xg_ref, ssq_ref):
    y = x_ref[...] + _dot(a_ref[...], w_ref[...])
    o_ref[...] = y
    xg_ref[...] = (y * g_ref[...]).astype(xg_ref.dtype)
    part = jnp.broadcast_to(jnp.sum(y * y, axis=-1, keepdims=True), ssq_ref.shape)

    @pl.when(pl.program_id(1) == 0)
    def _():
        ssq_ref[...] = part

    @pl.when(pl.program_id(1) > 0)
    def _():
        ssq_ref[...] += part


def _mm_residual(a, w, x, next_gain=None, next_layer=0):
    s, k = a.shape
    n = w.shape[1]
    bm, bn = min(ROW_TILE, s), min(COL_TILE, n)
    in_specs = [pl.BlockSpec((bm, k), lambda i, j: (i, 0)),
                pl.BlockSpec((k, bn), lambda i, j: (0, j)),
                pl.BlockSpec((bm, bn), lambda i, j: (i, j))]
    tile = pl.BlockSpec((bm, bn), lambda i, j: (i, j))
    if next_gain is None:
        return pl.pallas_call(
            _mm_res_kernel,
            grid=(s // bm, n // bn),
            in_specs=in_specs,
            out_specs=tile,
            out_shape=jax.ShapeDtypeStruct((s, n), F32),
            compiler_params=_params("parallel", "arbitrary"),
            name="mm_residual",
        )(a, w, x)
    return pl.pallas_call(
        _mm_res_norm_kernel,
        grid=(s // bm, n // bn),
        in_specs=in_specs + [pl.BlockSpec((None, 1, bn), lambda i, j: (next_layer, 0, j))],
        out_specs=[tile, tile, pl.BlockSpec((bm, SSQ_LANES), lambda i, j: (i, 0))],
        out_shape=[jax.ShapeDtypeStruct((s, n), F32),
                   jax.ShapeDtypeStruct((s, n), BF16),
                   jax.ShapeDtypeStruct((s, SSQ_LANES), F32)],
        compiler_params=_params("parallel", "arbitrary"),
        name="mm_residual_norm",
    )(a, w, x, next_gain.reshape(next_gain.shape[0], 1, n))


def _mm_kernel(xg_ref, ssq_ref, w_ref, o_ref):
    xg = xg_ref[...]
    o_ref[...] = (_row_factor(ssq_ref, xg.shape[1]) * _dot(xg, w_ref[...])).astype(o_ref.dtype)


def _mm(xg, ssq, w, n_out, bn, col_block_of, out_dtype):
    s, k = xg.shape
    bm = min(ROW_TILE, s)
    return pl.pallas_call(
        _mm_kernel,
        grid=(s // bm, n_out // bn),
        in_specs=[pl.BlockSpec((bm, k), lambda i, j: (i, 0)),
                  pl.BlockSpec((bm, SSQ_LANES), lambda i, j: (i, 0)),
                  pl.BlockSpec((k, bn), lambda i, j: (0, col_block_of(j)))],
        out_specs=pl.BlockSpec((bm, bn), lambda i, j: (i, j)),
        out_shape=jax.ShapeDtypeStruct((s, n_out), out_dtype),
        compiler_params=_params("parallel", "arbitrary"),
        name="in_proj",
    )(xg, ssq, w)


def _in_proj(xg, ssq, w_in, w_mix):
    bn = min(PROJ_COL_TILE, w_mix)
    nblk = w_mix // bn
    proj = _mm(xg, ssq, w_in, 9 * w_mix, bn, lambda j: j + jnp.where(j >= 3 * nblk, nblk, 0), BF16)
    f_pre = _mm(xg, ssq, w_in, w_mix, bn, lambda j: j + 3 * nblk, F32)
    return proj, f_pre


def _qk_norm_kernel(x_ref, g_ref, o_ref, *, n_heads):
    g = g_ref[...]
    for h in range(n_heads):
        x = x_ref[:, _hsl(h)].astype(F32)
        ms = jnp.mean(x * x, axis=-1, keepdims=True)
        o_ref[:, _hsl(h)] = (x * lax.rsqrt(ms + EPS) * g).astype(o_ref.dtype)


def _qk_norm(proj, gains, layer, w_mix):
    s = proj.shape[0]
    bm = min(512, s)
    return pl.pallas_call(
        functools.partial(_qk_norm_kernel, n_heads=w_mix // HEAD_DIM),
        grid=(s // bm, 2),
        in_specs=[pl.BlockSpec((bm, w_mix), lambda i, g: (i, g)),
                  pl.BlockSpec((None, None, 1, HEAD_DIM), lambda i, g: (layer, g, 0, 0))],
        out_specs=pl.BlockSpec((bm, w_mix), lambda i, g: (i, g)),
        out_shape=jax.ShapeDtypeStruct((s, 2 * w_mix), BF16),
        compiler_params=_params("parallel", "arbitrary"),
        name="qk_norm",
    )(proj, gains)


def _sb_kernel(*refs, tile, hps, factors):
    (q_ref, k_ref, v_ref), side_in, (o_ref,), side_out, (acc_ref, run_ref, nz_ref) = _split_refs(
        refs, 3, 1, len(factors))
    _cast_sides(side_in, side_out, factors)
    qi = pl.program_id(1)
    row = lax.broadcasted_iota(jnp.int32, (tile, tile), 0)
    col = lax.broadcasted_iota(jnp.int32, (tile, tile), 1)
    later = (row > col).astype(BF16)
    causal = col < row
    heads = range(hps)

    acc_ref[...] = jnp.zeros_like(acc_ref)
    run_ref[...] = jnp.zeros_like(run_ref)

    def logits(kb, hh):
        off = pl.multiple_of(kb * tile, tile)
        nz_ref[hh] = _dot_nt(q_ref[:, _hsl(hh)], k_ref[pl.ds(off, tile), _hsl(hh)])

    def block(kb, masked, kb_next):
        off = pl.multiple_of(kb * tile, tile)

        def stage_softplus(hh):
            nz = nz_ref[hh]
            neg_abs = lax.bitcast_convert_type(
                lax.bitcast_convert_type(nz, jnp.uint32) | jnp.uint32(0x80000000), F32)
            log_1m = jnp.minimum(nz, 0.0) - jnp.log2(1.0 + jnp.exp2(neg_abs))
            log_beta = log_1m - nz
            if masked:
                log_1m = jnp.where(causal, log_1m, 0.0)
            between = _dot(log_1m.astype(BF16), later)
            return log_beta, between, jnp.sum(log_1m, axis=-1, keepdims=True)

        def stage_out(hh, log_beta, between, total):
            att = jnp.exp2(log_beta + between)
            if masked:
                att = jnp.where(causal, att, 0.0)
            run = run_ref[hh]
            acc_ref[hh] += jnp.exp2(run) * _dot(att.astype(BF16), v_ref[pl.ds(off, tile), _hsl(hh)])
            run_ref[hh] = run + total

        mids = []
        for hh in heads:
            mids.append(stage_softplus(hh))
            logits(kb_next, hh)
        for hh in heads:
            stage_out(hh, *mids[hh])

    for hh in heads:
        logits(qi, hh)
    block(qi, True, jnp.maximum(qi - 1, 0))

    def stick_left():
        top = run_ref[0]
        for hh in range(1, hps):
            top = jnp.maximum(top, run_ref[hh])
        return jnp.max(top) > STICK_FLOOR_LOG2

    def body(state):
        it, _ = state
        kb = qi - 1 - it
        block(kb, False, jnp.maximum(kb - 1, 0))
        return it + 1, stick_left()

    lax.while_loop(lambda st: jnp.logical_and(st[0] < qi, st[1]), body, (jnp.int32(0), stick_left()))
    for hh in heads:
        o_ref[:, _hsl(hh)] = acc_ref[hh].astype(o_ref.dtype)


def _sb_attention(qk, proj, w_mix, sides=()):
    s = qk.shape[0]
    h = w_mix // HEAD_DIM
    hps = min(ATT_HEADS_PER_STEP, h)
    hg = h // hps
    wide = hps * HEAD_DIM
    tile = min(ATT_TILE, s)
    grid = (hg, s // tile)
    side_in, side_out, side_shapes = _side_specs(sides, grid)
    return pl.pallas_call(
        functools.partial(_sb_kernel, tile=tile, hps=hps, factors=tuple(f for _, _, f in sides)),
        grid=grid,
        in_specs=[pl.BlockSpec((tile, wide), lambda g, i: (i, g)),
                  pl.BlockSpec((s, wide), lambda g, i: (0, hg + g)),
                  pl.BlockSpec((s, wide), lambda g, i: (0, 2 * hg + g))] + side_in,
        out_specs=[pl.BlockSpec((tile, wide), lambda g, i: (i, g))] + side_out,
        out_shape=[jax.ShapeDtypeStruct((s, w_mix), BF16)] + side_shapes,
        scratch_shapes=[pltpu.VMEM((hps, tile, HEAD_DIM), F32),
                        pltpu.VMEM((hps, tile, 1), F32),
                        pltpu.VMEM((hps, tile, tile), F32)],
        compiler_params=_params("arbitrary", "arbitrary"),
        name="sb_attention",
    )(qk, qk, proj, *[src for src, _, _ in sides])


def _hgrn2_kernel(*refs, layer, rows, hps, factors):
    (f_ref, v_ref, q_ref, g_ref, lbl_ref, gain_ref), side_in, (o_ref,), side_out, (st_ref,) = _split_refs(
        refs, 6, 1, len(factors))
    _cast_sides(side_in, side_out, factors)

    @pl.when(pl.program_id(1) == 0)
    def _():
        st_ref[...] = jnp.zeros_like(st_ref)

    logits = lbl_ref[...]
    e = jnp.exp(logits - jnp.max(logits, axis=0, keepdims=True))
    p = e / jnp.sum(e, axis=0, keepdims=True)
    lb_all = jnp.zeros((1, hps * HEAD_DIM), F32)
    for i in range(1, layer + 1):
        lb_all = lb_all + p[i:i + 1, :]

    L = REC_CHUNK
    r_i = lax.broadcasted_iota(jnp.int32, (L, L), 0)
    c_i = lax.broadcasted_iota(jnp.int32, (L, L), 1)
    tri_incl = (c_i <= r_i).astype(BF16)
    half_row = lax.broadcasted_iota(jnp.int32, (SUB // 2, HEAD_DIM), 0)

    def chunk(c, carry):
        rs = pl.ds(pl.multiple_of(c * L, L), L)

        def stage_gates(hh):
            lb = lb_all[:, _hsl(hh)]
            log_lb = jnp.log(jnp.maximum(lb, LB_FLOOR))
            x = f_ref[rs, _hsl(hh)]
            b = jnp.log1p(-lb) + _log_sigmoid(x)
            lf = jnp.maximum(log_lb, b) + jnp.log(1.0 + jnp.exp(-jnp.abs(log_lb - b)))
            kk = (1.0 - lb) * jax.nn.sigmoid(-x)
            qq = _silu(q_ref[rs, _hsl(hh)].astype(F32))
            hi, mid, lo = _split3(lf)
            cum = (_dot(tri_incl, hi) + _dot(tri_incl, mid) + _dot(tri_incl, lo)) * LOG2_E
            return kk, qq, cum

        def stage_mix(hh, kk, qq, cum):
            v = v_ref[rs, _hsl(hh)]
            vf = v.astype(F32)
            last = cum[L - 1:L, :]

            st = st_ref[hh]
            out = _dot_nt((qq * jnp.exp2(cum)).astype(BF16), st.astype(BF16))

            a_rows = [jnp.zeros((SUB, L), F32)]
            for i in range(1, L // SUB):
                lo_r, hi_r = i * SUB, (i + 1) * SUB
                anchor = cum[lo_r - 1:lo_r, :]
                qt = qq[lo_r:hi_r, :] * jnp.exp2(cum[lo_r:hi_r, :] - anchor)
                kt = kk[:lo_r, :] * jnp.exp2(anchor - cum[:lo_r, :])
                kt = jnp.concatenate([kt, jnp.zeros((L - lo_r, HEAD_DIM), F32)], axis=0)
                a_rows.append(_dot_nt(qt.astype(BF16), kt.astype(BF16)))
            a_off = jnp.concatenate(a_rows, axis=0)
            out = out + _dot(a_off.astype(BF16), v)

            diag = []
            for i in range(L // SUB):
                base = i * SUB
                halves = []
                for r0 in (0, SUB // 2):
                    rows_r = slice(base + r0, base + r0 + SUB // 2)
                    cb, qb = cum[rows_r, :], qq[rows_r, :]
                    o_half = jnp.zeros((SUB // 2, HEAD_DIM), F32)
                    for s_ in range(r0 + SUB // 2):
                        srow = slice(base + s_, base + s_ + 1)
                        diff = cb - cum[srow, :]
                        if s_ > r0:
                            diff = jnp.where(half_row >= s_ - r0, diff, NEG_BIG)
                        a_col = jnp.sum(qb * jnp.exp2(diff) * kk[srow, :], axis=-1, keepdims=True)
                        o_half = o_half + a_col * vf[srow, :]
                    halves.append(o_half)
                diag.extend(halves)
            out = out + jnp.concatenate(diag, axis=0)

            kt = kk * jnp.exp2(last - cum)
            st_ref[hh] = jnp.exp2(last) * st + _dot(vf.T.astype(BF16), kt.astype(BF16))

            ms = jnp.mean(out * out, axis=-1, keepdims=True)
            y = out * lax.rsqrt(ms + EPS) * gain_ref[...]
            o_ref[rs, _hsl(hh)] = (y * _silu(g_ref[rs, _hsl(hh)].astype(F32))).astype(o_ref.dtype)

        gated = [stage_gates(hh) for hh in range(hps)]
        for hh in range(hps):
            stage_mix(hh, *gated[hh])
        return carry

    lax.fori_loop(0, rows // L, chunk, 0)


def _hgrn2(f_pre, proj, lb_logits, out_gain, w_mix, layer, sides=()):
    s = f_pre.shape[0]
    h = w_mix // HEAD_DIM
    hps = min(HGRN_HEADS_PER_STEP, h)
    hg = h // hps
    wide = hps * HEAD_DIM
    rows = min(REC_ROWS, s)
    depth = lb_logits.shape[0]
    blk = lambda grp: pl.BlockSpec((rows, wide), lambda g, c: (c, grp * hg + g))
    grid = (hg, s // rows)
    side_in, side_out, side_shapes = _side_specs(sides, grid)
    return pl.pallas_call(
        functools.partial(_hgrn2_kernel, layer=layer, rows=rows, hps=hps,
                          factors=tuple(f for _, _, f in sides)),
        grid=grid,
        in_specs=[pl.BlockSpec((rows, wide), lambda g, c: (c, g)),
                  blk(3), blk(4), blk(5),
                  pl.BlockSpec((depth, wide), lambda g, c: (0, g)),
                  pl.BlockSpec((None, 1, HEAD_DIM), lambda g, c: (layer, 0, 0))] + side_in,
        out_specs=[pl.BlockSpec((rows, wide), lambda g, c: (c, g))] + side_out,
        out_shape=[jax.ShapeDtypeStruct((s, w_mix), BF16)] + side_shapes,
        scratch_shapes=[pltpu.VMEM((hps, HEAD_DIM, HEAD_DIM), F32)],
        compiler_params=_params("arbitrary", "arbitrary"),
        name="hgrn2",
    )(f_pre, proj, proj, proj, lb_logits, out_gain.reshape(out_gain.shape[0], 1, HEAD_DIM),
      *[src for src, _, _ in sides])


def _mlstm_pre_kernel(x_ref, prev_ref, v_ref, cw_ref, cb_ref, wq_ref, wk_ref, wif_ref, bif_ref,
                      xc_ref, q_ref, k_ref, gates_ref, *, n_heads, conv_k):
    i = pl.program_id(0)
    x = x_ref[...].astype(F32)
    prev = prev_ref[8:16, :].astype(F32)
    prev = jnp.where(i > 0, prev, 0.0)
    row8 = lax.broadcasted_iota(jnp.int32, prev.shape, 0)
    acc = x * cw_ref[conv_k - 1:conv_k, :] + cb_ref[...]
    for back in range(1, conv_k):
        sh = pltpu.roll(x, back, 0)
        head = jnp.where(row8 < back, pltpu.roll(prev, back, 0), sh[:8, :])
        sh = jnp.concatenate([head, sh[8:, :]], axis=0)
        acc = acc + sh * cw_ref[conv_k - 1 - back:conv_k - back, :]
    xc = _silu(acc)
    xc_ref[...] = xc.astype(xc_ref.dtype)
    xcb = xc.astype(BF16)
    scale = HEAD_DIM ** -0.5
    w_mix = n_heads * HEAD_DIM
    g = jnp.zeros(gates_ref.shape, F32) + bif_ref[...]
    for h in range(n_heads):
        qh = _dot(xcb[:, _hsl(h)], wq_ref[h])
        kh = _dot(xcb[:, _hsl(h)], wk_ref[h])
        q_ref[:, _hsl(h)] = qh.astype(q_ref.dtype)
        k_ref[:, _hsl(h)] = (kh * scale).astype(k_ref.dtype)
        g = g + _dot_nt(wif_ref[:, _hsl(h)], qh.astype(BF16))
        g = g + _dot_nt(wif_ref[:, w_mix + h * HEAD_DIM:w_mix + (h + 1) * HEAD_DIM], kh.astype(BF16))
    g = g + _dot_nt(wif_ref[:, 2 * w_mix:], v_ref[...])
    gates_ref[...] = g


def _mlstm_pre(proj, conv_w, conv_b, w_q, w_k, w_if_t, b_if, layer, w_mix):
    s = proj.shape[0]
    h = w_mix // HEAD_DIM
    bm = min(512, s)
    conv_k = conv_w.shape[1]
    nl = conv_w.shape[0]
    lay3 = lambda i: (layer, 0, 0)
    return pl.pallas_call(
        functools.partial(_mlstm_pre_kernel, n_heads=h, conv_k=conv_k),
        grid=(s // bm,),
        in_specs=[pl.BlockSpec((bm, w_mix), lambda i: (i, 6)),
                  pl.BlockSpec((16, w_mix), lambda i: (jnp.maximum(i * (bm // 16) - 1, 0), 6)),
                  pl.BlockSpec((bm, w_mix), lambda i: (i, 7)),
                  pl.BlockSpec((None, conv_k, w_mix), lay3),
                  pl.BlockSpec((None, 1, w_mix), lay3),
                  pl.BlockSpec((None, h, HEAD_DIM, HEAD_DIM), lambda i: (layer, 0, 0, 0)),
                  pl.BlockSpec((None, h, HEAD_DIM, HEAD_DIM), lambda i: (layer, 0, 0, 0)),
                  pl.BlockSpec((None, 2 * h, 3 * w_mix), lay3),
                  pl.BlockSpec((None, 2 * h, 1), lay3)],
        out_specs=[pl.BlockSpec((bm, w_mix), lambda i: (i, 0)),
                   pl.BlockSpec((bm, w_mix), lambda i: (i, 0)),
                   pl.BlockSpec((bm, w_mix), lambda i: (i, 0)),
                   pl.BlockSpec((2 * h, bm), lambda i: (0, i))],
        out_shape=[jax.ShapeDtypeStruct((s, w_mix), BF16),
                   jax.ShapeDtypeStruct((s, w_mix), BF16),
                   jax.ShapeDtypeStruct((s, w_mix), BF16),
                   jax.ShapeDtypeStruct((2 * h, s), F32)],
        compiler_params=_params("parallel"),
        name="mlstm_pre",
    )(proj, proj, proj, conv_w, conv_b.reshape(nl, 1, w_mix), w_q, w_k, w_if_t, b_if.reshape(nl, 2 * h, 1))


def _mlstm_kernel(*refs, rows, hps, factors):
    ((q_ref, k_ref, v_ref, li_ref, gf_ref, xc_ref, z_ref, skip_ref, gain_ref), side_in, (o_ref,), side_out,
     (c_ref, n_ref, m_ref)) = _split_refs(refs, 9, 1, len(factors))
    _cast_sides(side_in, side_out, factors)

    @pl.when(pl.program_id(1) == 0)
    def _():
        c_ref[...] = jnp.zeros_like(c_ref)
        n_ref[...] = jnp.zeros_like(n_ref)
        m_ref[...] = jnp.zeros_like(m_ref)

    L = REC_CHUNK
    r_i = lax.broadcasted_iota(jnp.int32, (L, L), 0)
    c_i = lax.broadcasted_iota(jnp.int32, (L, L), 1)
    upto = (r_i <= c_i).astype(BF16)
    causal = c_i <= r_i

    def chunk(c, carry):
        rs = pl.ds(pl.multiple_of(c * L, L), L)
        heads = range(hps)

        def stage_gates(hh):
            li = li_ref[0, hh, :, rs]
            lf = _log_sigmoid(gf_ref[0, hh, :, rs])
            hi, mid, lo = _split3(jnp.broadcast_to(lf, (8, L)))
            cum = (_dot(hi, upto) + _dot(mid, upto) + _dot(lo, upto))[0:1, :]
            return li, cum

        def stage_weights(hh, li, cum):
            q = q_ref[rs, _hsl(hh)]
            k = k_ref[rs, _hsl(hh)]
            cum_t = jnp.broadcast_to(cum, (L, L)).T
            m_old = m_ref[hh]
            dmat = jnp.where(causal, cum_t - cum + li, NEG_BIG)
            inter = cum_t[:, 0:1] + m_old
            m_t = jnp.maximum(inter, jnp.max(dmat, axis=-1, keepdims=True))
            w = jnp.exp(dmat - m_t) * _dot_nt(q, k)
            return w, jnp.exp(inter - m_t), m_t

        def stage_output(hh, w, carry_w, m_t):
            q = q_ref[rs, _hsl(hh)]
            v = v_ref[rs, _hsl(hh)]
            num = _dot(w.astype(BF16), v) + carry_w * _dot(q, c_ref[hh].astype(BF16))
            den = (jnp.sum(w, axis=-1, keepdims=True)
                   + carry_w * jnp.sum(q.astype(F32) * n_ref[hh], axis=-1, keepdims=True))
            h_out = num / jnp.maximum(jnp.abs(den), jnp.exp(-m_t))
            ms = jnp.mean(h_out * h_out, axis=-1, keepdims=True)
            hn = h_out * lax.rsqrt(ms + EPS) * gain_ref[...]
            y = ((hn + skip_ref[:, _hsl(hh)] * xc_ref[rs, _hsl(hh)].astype(F32))
                 * _silu(z_ref[rs, _hsl(hh)].astype(F32)))
            o_ref[rs, _hsl(hh)] = y.astype(o_ref.dtype)

        def stage_state(hh, li, cum):
            k = k_ref[rs, _hsl(hh)]
            v = v_ref[rs, _hsl(hh)]
            m_old = m_ref[hh]
            last = cum[:, L - 1:L]
            g_row = last - cum + li
            m_new = jnp.maximum(last + m_old, jnp.max(g_row, axis=-1, keepdims=True))
            wk = jnp.exp(g_row - m_new)
            kw = k.astype(F32) * jnp.broadcast_to(wk, (L, L)).T
            decay = jnp.exp(last + m_old - m_new)
            c_ref[hh] = decay * c_ref[hh] + _dot(kw.T.astype(BF16), v)
            n_ref[hh] = decay * n_ref[hh] + jnp.sum(kw, axis=0, keepdims=True)
            m_ref[hh] = m_new

        gates = [stage_gates(hh) for hh in heads]
        weights = [stage_weights(hh, *gates[hh]) for hh in heads]
        for hh in heads:
            stage_output(hh, *weights[hh])
        for hh in heads:
            stage_state(hh, *gates[hh])
        return carry

    lax.fori_loop(0, rows // L, chunk, 0)


def _mlstm(q_c, k_c, proj, gates, xc, skip, out_gain, layer, w_mix, sides=()):
    s = q_c.shape[0]
    h = w_mix // HEAD_DIM
    hps = min(MLSTM_HEADS_PER_STEP, h)
    hg = h // hps
    wide = hps * HEAD_DIM
    rows = min(REC_ROWS, s)
    nl = skip.shape[0]
    gates4 = gates.reshape(2, h, 1, s)
    hd = lambda: pl.BlockSpec((rows, wide), lambda g, c: (c, g))
    grid = (hg, s // rows)
    side_in, side_out, side_shapes = _side_specs(sides, grid)
    return pl.pallas_call(
        functools.partial(_mlstm_kernel, rows=rows, hps=hps, factors=tuple(f for _, _, f in sides)),
        grid=grid,
        in_specs=[hd(), hd(),
                  pl.BlockSpec((rows, wide), lambda g, c: (c, 7 * hg + g)),
                  pl.BlockSpec((1, hps, 1, rows), lambda g, c: (0, g, 0, c)),
                  pl.BlockSpec((1, hps, 1, rows), lambda g, c: (1, g, 0, c)),
                  hd(),
                  pl.BlockSpec((rows, wide), lambda g, c: (c, 8 * hg + g)),
                  pl.BlockSpec((None, 1, wide), lambda g, c: (layer, 0, g)),
                  pl.BlockSpec((None, 1, HEAD_DIM), lambda g, c: (layer, 0, 0))] + side_in,
        out_specs=[hd()] + side_out,
        out_shape=[jax.ShapeDtypeStruct((s, w_mix), BF16)] + side_shapes,
        scratch_shapes=[pltpu.VMEM((hps, HEAD_DIM, HEAD_DIM), F32),
                        pltpu.VMEM((hps, 1, HEAD_DIM), F32),
                        pltpu.VMEM((hps, 1, 1), F32)],
        compiler_params=_params("arbitrary", "arbitrary"),
        name="mlstm",
    )(q_c, k_c, proj, gates4, gates4, xc, proj, skip.reshape(nl, 1, w_mix),
      out_gain.reshape(nl, 1, HEAD_DIM), *[src for src, _, _ in sides])


def _merge_kernel(xg_ref, ssq_ref, wg_ref, ya_ref, yb_ref, yc_ref, wa_ref, wb_ref, wc_ref, o_ref):
    xg = xg_ref[...]
    r = _row_factor(ssq_ref, xg.shape[1])
    out = None
    for b, (y_ref, w_ref) in enumerate(((ya_ref, wa_ref), (yb_ref, wb_ref), (yc_ref, wc_ref))):
        gate = jax.nn.sigmoid(r * _dot(xg, wg_ref[b]))
        term = gate * _dot(y_ref[...], w_ref[...])
        out = term if out is None else out + term
    o_ref[...] = out.astype(o_ref.dtype)


def _merge(xg, ssq, w_gate, ys, w_branches):
    s, d = xg.shape
    w_mix = ys[0].shape[1]
    bm, bn = min(ROW_TILE, s), min(MERGE_COL_TILE, d)
    yspec = pl.BlockSpec((bm, w_mix), lambda i, j: (i, 0))
    wspec = pl.BlockSpec((w_mix, bn), lambda i, j: (0, j))
    return pl.pallas_call(
        _merge_kernel,
        grid=(s // bm, d // bn),
        in_specs=[pl.BlockSpec((bm, d), lambda i, j: (i, 0)),
                  pl.BlockSpec((bm, SSQ_LANES), lambda i, j: (i, 0)),
                  pl.BlockSpec((3, d, bn), lambda i, j: (0, 0, j)),
                  yspec, yspec, yspec, wspec, wspec, wspec],
        out_specs=pl.BlockSpec((bm, bn), lambda i, j: (i, j)),
        out_shape=jax.ShapeDtypeStruct((s, d), BF16),
        compiler_params=_params("parallel", "arbitrary"),
        name="merge",
    )(xg, ssq, w_gate, *ys, *w_branches)


def kernel(x, ffn1_norm, ffn1_w_gate, ffn1_w_up, ffn1_w_down, mix_norm, w_in, sb_q_gain, sb_k_gain,
           hg_lb_logits, hg_out_gain, ml_conv_w, ml_conv_b, ml_w_q, ml_w_k, ml_w_if, ml_b_if,
           ml_out_gain, ml_skip, w_merge_gate, w_branch_a, w_branch_b, w_branch_c, w_out,
           ffn2_norm, ffn2_w_gate, ffn2_w_up, ffn2_w_down):
    batch, seq, d_model = x.shape
    depth = w_in.shape[0]
    w_mix = w_branch_a.shape[1]
    scale = HEAD_DIM ** -0.5
    bf = lambda a: a.astype(BF16)

    gates_f32 = w_merge_gate.reshape(depth, 3 * d_model, d_model)
    first = (bf(ffn1_w_gate[0]), bf(ffn1_w_up[0]), bf(0.5 * ffn1_w_down[0]), bf(w_in[0]))
    wq_b, wk_b, wif_t = bf(ml_w_q), bf(ml_w_k), bf(jnp.swapaxes(ml_w_if, 1, 2))
    qk_gains = jnp.stack([sb_q_gain * (-scale * LOG2_E), sb_k_gain], axis=1).reshape(depth, 2, 1, HEAD_DIM)

    outs = []
    for bi in range(batch):
        xs = x[bi]
        f1g, f1u, f1d, w_in_l = first
        xg, ssq = _norm_prep(xs, ffn1_norm, 0)
        for l in range(depth):
            more = l + 1 < depth
            hid = _ffn_up(xg, ssq, f1g, f1u)
            xs, xg, ssq = _mm_residual(hid, f1d, xs, mix_norm, l)

            proj, f_pre = _in_proj(xg, ssq, w_in_l, w_mix)
            qk = _qk_norm(proj, qk_gains, l, w_mix)
            y_a, w_out_l, wb_a, wb_b, wb_c, *nxt_in = _sb_attention(
                qk, proj, w_mix,
                [(w_out, l, 1.0), (w_branch_a, l, 1.0), (w_branch_b, l, 1.0), (w_branch_c, l, 1.0)]
                + ([(w_in, l + 1, 1.0)] if more else []))
            y_b, gates_l, f2g, f2u = _hgrn2(
                f_pre, proj, hg_lb_logits, hg_out_gain, w_mix, l,
                [(gates_f32, l, 1.0), (ffn2_w_gate, l, 1.0), (ffn2_w_up, l, 1.0)])
            xc, q_c, k_c, gates = _mlstm_pre(proj, ml_conv_w, ml_conv_b, wq_b, wk_b, wif_t, ml_b_if, l, w_mix)
            y_c, f2d, *nxt_ffn = _mlstm(
                q_c, k_c, proj, gates, xc, ml_skip, ml_out_gain, l, w_mix,
                [(ffn2_w_down, l, 0.5)]
                + ([(ffn1_w_gate, l + 1, 1.0), (ffn1_w_up, l + 1, 1.0), (ffn1_w_down, l + 1, 0.5)] if more else []))
            merged = _merge(xg, ssq, gates_l.reshape(3, d_model, d_model), (y_a, y_b, y_c), (wb_a, wb_b, wb_c))
            xs, xg, ssq = _mm_residual(merged, w_out_l, xs, ffn2_norm, l)

            hid = _ffn_up(xg, ssq, f2g, f2u)
            if more:
                xs, xg, ssq = _mm_residual(hid, f2d, xs, ffn1_norm, l + 1)
                (f1g, f1u, f1d), (w_in_l,) = nxt_ffn, nxt_in
            else:
                xs = _mm_residual(hid, f2d, xs)
        outs.append(xs)
    return outs[0][None] if batch == 1 else jnp.stack(outs)
```

```python
import functools

import jax
import jax.numpy as jnp
from jax import lax
from jax.experimental import pallas as pl
from jax.experimental.pallas import tpu as pltpu

F32 = jnp.float32
BF16 = jnp.bfloat16

HEAD_DIM = 128
EPS = 1e-6
NEG_BIG = -1e30
LB_FLOOR = 1e-30
LOG2_E = 1.4426950408889634
STICK_FLOOR_LOG2 = -160.0

V7X_VMEM_LIMIT_BYTES = 56 * 1024 * 1024
ROW_TILE = 1024
COL_TILE = 512
PROJ_COL_TILE = 1024
RES_COL_TILE = 512
MERGE_COL_TILE = 256
SSQ_LANES = 128
ATT_TILE = 256
ATT_SLAB = 16
REC_CHUNK = 128
REC_ROWS = 512
SUB = 16
HGRN_HEADS_PER_STEP = 2
MLSTM_HEADS_PER_STEP = 4
ATT_HEADS_PER_STEP = 4


def _params(*sem):
    return pltpu.CompilerParams(dimension_semantics=sem, vmem_limit_bytes=V7X_VMEM_LIMIT_BYTES)


def _dot(a, b):
    return jnp.dot(a, b, preferred_element_type=F32)


def _dot_nt(a, b):
    return lax.dot_general(a, b, (((1,), (1,)), ((), ())), preferred_element_type=F32)


def _split3(x):
    hi = x.astype(BF16)
    r1 = x - hi.astype(F32)
    mid = r1.astype(BF16)
    lo = (r1 - mid.astype(F32)).astype(BF16)
    return hi, mid, lo


def _softplus(z):
    return jnp.maximum(z, 0.0) + jnp.log(1.0 + jnp.exp(-jnp.abs(z)))


def _log_sigmoid(x):
    return jnp.minimum(x, 0.0) - jnp.log(1.0 + jnp.exp(-jnp.abs(x)))


def _silu(x):
    return x * jax.nn.sigmoid(x)


def _hsl(hh):
    return slice(hh * HEAD_DIM, (hh + 1) * HEAD_DIM)


def _side_specs(sides, grid):
    steps = 1
    for n in grid:
        steps *= n

    def chunk(*g):
        idx = g[0]
        for a, n in zip(g[1:], grid[1:]):
            idx = idx * n + a
        return idx

    in_specs, out_specs, out_shapes = [], [], []
    for src, layer, _ in sides:
        _, r, c = src.shape
        rb = r // steps
        assert rb * steps == r and rb % 16 == 0, (src.shape, steps)
        in_specs.append(pl.BlockSpec((None, rb, c), lambda *g, layer=layer: (layer, chunk(*g), 0)))
        out_specs.append(pl.BlockSpec((rb, c), lambda *g: (chunk(*g), 0)))
        out_shapes.append(jax.ShapeDtypeStruct((r, c), BF16))
    return in_specs, out_specs, out_shapes


def _cast_sides(in_refs, out_refs, factors):
    for i_ref, o_ref, f in zip(in_refs, out_refs, factors):
        v = i_ref[...]
        o_ref[...] = (v if f == 1.0 else v * f).astype(o_ref.dtype)


def _split_refs(refs, n_in, n_out, n_side):
    a, b, c, d = n_in, n_in + n_side, n_in + n_side + n_out, n_in + 2 * n_side + n_out
    return refs[:a], refs[a:b], refs[b:c], refs[c:d], refs[d:]


def _row_factor(ssq_ref, d):
    return lax.rsqrt(ssq_ref[:, 0:1] * (1.0 / d) + EPS)


def _norm_prep_kernel(x_ref, g_ref, xg_ref, ssq_ref):
    x = x_ref[...]
    xg_ref[...] = (x * g_ref[...]).astype(xg_ref.dtype)
    ssq_ref[...] = jnp.broadcast_to(jnp.sum(x * x, axis=-1, keepdims=True), ssq_ref.shape)


def _norm_prep(x, gains, layer):
    s, d = x.shape
    bm = min(256, s)
    return pl.pallas_call(
        _norm_prep_kernel,
        grid=(s // bm,),
        in_specs=[pl.BlockSpec((bm, d), lambda i: (i, 0)),
                  pl.BlockSpec((None, 1, d), lambda i: (layer, 0, 0))],
        out_specs=[pl.BlockSpec((bm, d), lambda i: (i, 0)),
                   pl.BlockSpec((bm, SSQ_LANES), lambda i: (i, 0))],
        out_shape=[jax.ShapeDtypeStruct((s, d), BF16),
                   jax.ShapeDtypeStruct((s, SSQ_LANES), F32)],
        compiler_params=_params("parallel"),
        name="norm_prep",
    )(x, gains.reshape(gains.shape[0], 1, d))


def _ffn_up_kernel(xg_ref, ssq_ref, wg_ref, wu_ref, o_ref):
    xg = xg_ref[...]
    r = _row_factor(ssq_ref, xg.shape[1])
    g = r * _dot(xg, wg_ref[...])
    u = r * _dot(xg, wu_ref[...])
    o_ref[...] = (_silu(g) * u).astype(o_ref.dtype)


def _ffn_up(xg, ssq, wg, wu):
    s, d = xg.shape
    f = wg.shape[1]
    bm, bn = min(ROW_TILE, s), min(COL_TILE, f)
    wspec = pl.BlockSpec((d, bn), lambda i, j: (0, j))
    return pl.pallas_call(
        _ffn_up_kernel,
        grid=(s // bm, f // bn),
        in_specs=[pl.BlockSpec((bm, d), lambda i, j: (i, 0)),
                  pl.BlockSpec((bm, SSQ_LANES), lambda i, j: (i, 0)), wspec, wspec],
        out_specs=pl.BlockSpec((bm, bn), lambda i, j: (i, j)),
        out_shape=jax.ShapeDtypeStruct((s, f), BF16),
        compiler_params=_params("parallel", "arbitrary"),
        name="ffn_up",
    )(xg, ssq, wg, wu)


def _mm_res_kernel(a_ref, w_ref, x_ref, o_ref):
    o_ref[...] = x_ref[...] + _dot(a_ref[...], w_ref[...])


def _mm_res_norm_kernel(a_ref, w_ref, x_ref, g_ref, o_ref, xg_ref, ssq_ref):
    y = x_ref[...] + _dot(a_ref[...], w_ref[...])
    o_ref[...] = y
    xg_ref[...] = (y * g_ref[...]).astype(xg_ref.dtype)
    part = jnp.broadcast_to(jnp.sum(y * y, axis=-1, keepdims=True), ssq_ref.shape)

    @pl.when(pl.program_id(1) == 0)
    def _():
        ssq_ref[...] = part

    @pl.when(pl.program_id(1) > 0)
    def _():
        ssq_ref[...] += part


def _mm_residual(a, w, x, next_gain=None, next_layer=0):
    s, k = a.shape
    n = w.shape[1]
    bm, bn = min(ROW_TILE, s), min(RES_COL_TILE, n)
    in_specs = [pl.BlockSpec((bm, k), lambda i, j: (i, 0)),
                pl.BlockSpec((k, bn), lambda i, j: (0, j)),
                pl.BlockSpec((bm, bn), lambda i, j: (i, j))]
    tile = pl.BlockSpec((bm, bn), lambda i, j: (i, j))
    if next_gain is None:
        return pl.pallas_call(
            _mm_res_kernel,
            grid=(s // bm, n // bn),
            in_specs=in_specs,
            out_specs=tile,
            out_shape=jax.ShapeDtypeStruct((s, n), F32),
            compiler_params=_params("parallel", "arbitrary"),
            name="mm_residual",
        )(a, w, x)
    return pl.pallas_call(
        _mm_res_norm_kernel,
        grid=(s // bm, n // bn),
        in_specs=in_specs + [pl.BlockSpec((None, 1, bn), lambda i, j: (next_layer, 0, j))],
        out_specs=[tile, tile, pl.BlockSpec((bm, SSQ_LANES), lambda i, j: (i, 0))],
        out_shape=[jax.ShapeDtypeStruct((s, n), F32),
                   jax.ShapeDtypeStruct((s, n), BF16),
                   jax.ShapeDtypeStruct((s, SSQ_LANES), F32)],
        compiler_params=_params("parallel", "arbitrary"),
        name="mm_residual_norm",
    )(a, w, x, next_gain.reshape(next_gain.shape[0], 1, n))


def _mm_kernel(xg_ref, ssq_ref, w_ref, gain_ref, o_ref, *, qk_blocks):
    xg = xg_ref[...]
    y = _row_factor(ssq_ref, xg.shape[1]) * _dot(xg, w_ref[...])
    if qk_blocks == 0:
        o_ref[...] = y.astype(o_ref.dtype)
        return
    j = pl.program_id(1)

    @pl.when(j >= qk_blocks)
    def _():
        o_ref[...] = y.astype(o_ref.dtype)

    @pl.when(j < qk_blocks)
    def _():
        g = gain_ref[...]
        for h in range(y.shape[1] // HEAD_DIM):
            yh = y[:, _hsl(h)]
            ms = jnp.mean(yh * yh, axis=-1, keepdims=True)
            o_ref[:, _hsl(h)] = (yh * lax.rsqrt(ms + EPS) * g).astype(o_ref.dtype)


def _mm(xg, ssq, w, n_out, bn, col_block_of, out_dtype, qk_gains, layer, qk_blocks, nblk):
    s, k = xg.shape
    bm = min(ROW_TILE, s)
    return pl.pallas_call(
        functools.partial(_mm_kernel, qk_blocks=qk_blocks),
        grid=(s // bm, n_out // bn),
        in_specs=[pl.BlockSpec((bm, k), lambda i, j: (i, 0)),
                  pl.BlockSpec((bm, SSQ_LANES), lambda i, j: (i, 0)),
                  pl.BlockSpec((k, bn), lambda i, j: (0, col_block_of(j))),
                  pl.BlockSpec((None, None, 1, HEAD_DIM),
                               lambda i, j: (layer, jnp.minimum(j // nblk, 1), 0, 0))],
        out_specs=pl.BlockSpec((bm, bn), lambda i, j: (i, j)),
        out_shape=jax.ShapeDtypeStruct((s, n_out), out_dtype),
        compiler_params=_params("parallel", "arbitrary"),
        name="in_proj",
    )(xg, ssq, w, qk_gains)


def _in_proj(xg, ssq, w_in, w_mix, qk_gains, layer):
    bn = min(PROJ_COL_TILE, w_mix)
    nblk = w_mix // bn
    proj = _mm(xg, ssq, w_in, 9 * w_mix, bn, lambda j: j + jnp.where(j >= 3 * nblk, nblk, 0), BF16,
               qk_gains, layer, 2 * nblk, nblk)
    f_pre = _mm(xg, ssq, w_in, w_mix, bn, lambda j: j + 3 * nblk, F32, qk_gains, layer, 0, nblk)
    return proj, f_pre


def _sb_kernel(*refs, tile, hps, factors):
    (q_ref, k_ref, v_ref), side_in, (o_ref,), side_out, (acc_ref, run_ref, nz_ref) = _split_refs(
        refs, 3, 1, len(factors))
    _cast_sides(side_in, side_out, factors)
    qi = pl.program_id(1)
    row = lax.broadcasted_iota(jnp.int32, (tile, tile), 0)
    col = lax.broadcasted_iota(jnp.int32, (tile, tile), 1)
    later = (row > col).astype(BF16)
    causal = col < row
    heads = range(hps)

    acc_ref[...] = jnp.zeros_like(acc_ref)
    run_ref[...] = jnp.zeros_like(run_ref)

    def logits(kb, hh):
        off = pl.multiple_of(kb * tile, tile)
        nz_ref[hh] = _dot_nt(q_ref[:, _hsl(hh)], k_ref[pl.ds(off, tile), _hsl(hh)])

    def block(kb, masked, kb_next):
        off = pl.multiple_of(kb * tile, tile)

        def stage_softplus(hh):
            nz = nz_ref[hh]
            neg_abs = lax.bitcast_convert_type(
                lax.bitcast_convert_type(nz, jnp.uint32) | jnp.uint32(0x80000000), F32)
            log_1m = jnp.minimum(nz, 0.0) - jnp.log2(1.0 + jnp.exp2(neg_abs))
            log_beta = log_1m - nz
            if masked:
                log_1m = jnp.where(causal, log_1m, 0.0)
            between = _dot(log_1m.astype(BF16), later)
            return log_beta, between, jnp.sum(log_1m, axis=-1, keepdims=True)

        def stage_out(hh, log_beta, between, total):
            att = jnp.exp2(log_beta + between)
            if masked:
                att = jnp.where(causal, att, 0.0)
            run = run_ref[hh]
            acc_ref[hh] += jnp.exp2(run) * _dot(att.astype(BF16), v_ref[pl.ds(off, tile), _hsl(hh)])
            run_ref[hh] = run + total

        mids = []
        for hh in heads:
            mids.append(stage_softplus(hh))
            logits(kb_next, hh)
        for hh in heads:
            stage_out(hh, *mids[hh])

    for hh in heads:
        logits(qi, hh)
    block(qi, True, jnp.maximum(qi - 1, 0))

    def stick_left():
        top = run_ref[0]
        for hh in range(1, hps):
            top = jnp.maximum(top, run_ref[hh])
        return jnp.max(top) > STICK_FLOOR_LOG2

    def body(state):
        it, _ = state
        kb = qi - 1 - it
        block(kb, False, jnp.maximum(kb - 1, 0))
        return it + 1, stick_left()

    lax.while_loop(lambda st: jnp.logical_and(st[0] < qi, st[1]), body, (jnp.int32(0), stick_left()))
    for hh in heads:
        o_ref[:, _hsl(hh)] = acc_ref[hh].astype(o_ref.dtype)


def _sb_attention(proj, w_mix, sides=()):
    s = proj.shape[0]
    h = w_mix // HEAD_DIM
    hps = min(ATT_HEADS_PER_STEP, h)
    hg = h // hps
    wide = hps * HEAD_DIM
    tile = min(ATT_TILE, s)
    grid = (hg, s // tile)
    side_in, side_out, side_shapes = _side_specs(sides, grid)
    return pl.pallas_call(
        functools.partial(_sb_kernel, tile=tile, hps=hps, factors=tuple(f for _, _, f in sides)),
        grid=grid,
        in_specs=[pl.BlockSpec((tile, wide), lambda g, i: (i, g)),
                  pl.BlockSpec((s, wide), lambda g, i: (0, hg + g)),
                  pl.BlockSpec((s, wide), lambda g, i: (0, 2 * hg + g))] + side_in,
        out_specs=[pl.BlockSpec((tile, wide), lambda g, i: (i, g))] + side_out,
        out_shape=[jax.ShapeDtypeStruct((s, w_mix), BF16)] + side_shapes,
        scratch_shapes=[pltpu.VMEM((hps, tile, HEAD_DIM), F32),
                        pltpu.VMEM((hps, tile, 1), F32),
                        pltpu.VMEM((hps, tile, tile), F32)],
        compiler_params=_params("arbitrary", "arbitrary"),
        name="sb_attention",
    )(proj, proj, proj, *[src for src, _, _ in sides])


def _hgrn2_kernel(*refs, layer, rows, hps, factors):
    (f_ref, v_ref, q_ref, g_ref, lbl_ref, gain_ref), side_in, (o_ref,), side_out, (st_ref,) = _split_refs(
        refs, 6, 1, len(factors))
    _cast_sides(side_in, side_out, factors)

    @pl.when(pl.program_id(1) == 0)
    def _():
        st_ref[...] = jnp.zeros_like(st_ref)

    logits = lbl_ref[...]
    e = jnp.exp(logits - jnp.max(logits, axis=0, keepdims=True))
    p = e / jnp.sum(e, axis=0, keepdims=True)
    lb_all = jnp.zeros((1, hps * HEAD_DIM), F32)
    for i in range(1, layer + 1):
        lb_all = lb_all + p[i:i + 1, :]

    L = REC_CHUNK
    r_i = lax.broadcasted_iota(jnp.int32, (L, L), 0)
    c_i = lax.broadcasted_iota(jnp.int32, (L, L), 1)
    tri_incl = (c_i <= r_i).astype(BF16)
    half_row = lax.broadcasted_iota(jnp.int32, (SUB // 2, HEAD_DIM), 0)

    def chunk(c, carry):
        rs = pl.ds(pl.multiple_of(c * L, L), L)

        def stage_gates(hh):
            lb = lb_all[:, _hsl(hh)]
            log_lb = jnp.log(jnp.maximum(lb, LB_FLOOR))
            x = f_ref[rs, _hsl(hh)]
            b = jnp.log1p(-lb) + _log_sigmoid(x)
            lf = jnp.maximum(log_lb, b) + jnp.log(1.0 + jnp.exp(-jnp.abs(log_lb - b)))
            kk = (1.0 - lb) * jax.nn.sigmoid(-x)
            qq = _silu(q_ref[rs, _hsl(hh)].astype(F32))
            hi, mid, lo = _split3(lf)
            cum = (_dot(tri_incl, hi) + _dot(tri_incl, mid) + _dot(tri_incl, lo)) * LOG2_E
            return kk, qq, cum

        def stage_mix(hh, kk, qq, cum):
            v = v_ref[rs, _hsl(hh)]
            vf = v.astype(F32)
            last = cum[L - 1:L, :]

            st = st_ref[hh]
            out = _dot_nt((qq * jnp.exp2(cum)).astype(BF16), st.astype(BF16))

            a_rows = [jnp.zeros((SUB, L), F32)]
            for i in range(1, L // SUB):
                lo_r, hi_r = i * SUB, (i + 1) * SUB
                anchor = cum[lo_r - 1:lo_r, :]
                qt = qq[lo_r:hi_r, :] * jnp.exp2(cum[lo_r:hi_r, :] - anchor)
                kt = kk[:lo_r, :] * jnp.exp2(anchor - cum[:lo_r, :])
                kt = jnp.concatenate([kt, jnp.zeros((L - lo_r, HEAD_DIM), F32)], axis=0)
                a_rows.append(_dot_nt(qt.astype(BF16), kt.astype(BF16)))
            a_off = jnp.concatenate(a_rows, axis=0)
            out = out + _dot(a_off.astype(BF16), v)

            diag = []
            for i in range(L // SUB):
                base = i * SUB
                halves = []
                for r0 in (0, SUB // 2):
                    rows_r = slice(base + r0, base + r0 + SUB // 2)
                    cb, qb = cum[rows_r, :], qq[rows_r, :]
                    o_half = jnp.zeros((SUB // 2, HEAD_DIM), F32)
                    for s_ in range(r0 + SUB // 2):
                        srow = slice(base + s_, base + s_ + 1)
                        diff = cb - cum[srow, :]
                        if s_ > r0:
                            diff = jnp.where(half_row >= s_ - r0, diff, NEG_BIG)
                        a_col = jnp.sum(qb * jnp.exp2(diff) * kk[srow, :], axis=-1, keepdims=True)
                        o_half = o_half + a_col * vf[srow, :]
                    halves.append(o_half)
                diag.extend(halves)
            out = out + jnp.concatenate(diag, axis=0)

            kt = kk * jnp.exp2(last - cum)
            st_ref[hh] = jnp.exp2(last) * st + _dot(vf.T.astype(BF16), kt.astype(BF16))

            ms = jnp.mean(out * out, axis=-1, keepdims=True)
            y = out * lax.rsqrt(ms + EPS) * gain_ref[...]
            o_ref[rs, _hsl(hh)] = (y * _silu(g_ref[rs, _hsl(hh)].astype(F32))).astype(o_ref.dtype)

        gated = [stage_gates(hh) for hh in range(hps)]
        for hh in range(hps):
            stage_mix(hh, *gated[hh])
        return carry

    lax.fori_loop(0, rows // L, chunk, 0)


def _hgrn2(f_pre, proj, lb_logits, out_gain, w_mix, layer, sides=()):
    s = f_pre.shape[0]
    h = w_mix // HEAD_DIM
    hps = min(HGRN_HEADS_PER_STEP, h)
    hg = h // hps
    wide = hps * HEAD_DIM
    rows = min(REC_ROWS, s)
    depth = lb_logits.shape[0]
    blk = lambda grp: pl.BlockSpec((rows, wide), lambda g, c: (c, grp * hg + g))
    grid = (hg, s // rows)
    side_in, side_out, side_shapes = _side_specs(sides, grid)
    return pl.pallas_call(
        functools.partial(_hgrn2_kernel, layer=layer, rows=rows, hps=hps,
                          factors=tuple(f for _, _, f in sides)),
        grid=grid,
        in_specs=[pl.BlockSpec((rows, wide), lambda g, c: (c, g)),
                  blk(3), blk(4), blk(5),
                  pl.BlockSpec((depth, wide), lambda g, c: (0, g)),
                  pl.BlockSpec((None, 1, HEAD_DIM), lambda g, c: (layer, 0, 0))] + side_in,
        out_specs=[pl.BlockSpec((rows, wide), lambda g, c: (c, g))] + side_out,
        out_shape=[jax.ShapeDtypeStruct((s, w_mix), BF16)] + side_shapes,
        scratch_shapes=[pltpu.VMEM((hps, HEAD_DIM, HEAD_DIM), F32)],
        compiler_params=_params("arbitrary", "arbitrary"),
        name="hgrn2",
    )(f_pre, proj, proj, proj, lb_logits, out_gain.reshape(out_gain.shape[0], 1, HEAD_DIM),
      *[src for src, _, _ in sides])


def _mlstm_pre_kernel(x_ref, prev_ref, v_ref, cw_ref, cb_ref, wq_ref, wk_ref, wif_ref, bif_ref,
                      xc_ref, q_ref, k_ref, gates_ref, *, n_heads, conv_k):
    i = pl.program_id(0)
    x = x_ref[...].astype(F32)
    prev = prev_ref[8:16, :].astype(F32)
    prev = jnp.where(i > 0, prev, 0.0)
    row8 = lax.broadcasted_iota(jnp.int32, prev.shape, 0)
    acc = x * cw_ref[conv_k - 1:conv_k, :] + cb_ref[...]
    for back in range(1, conv_k):
        sh = pltpu.roll(x, back, 0)
        head = jnp.where(row8 < back, pltpu.roll(prev, back, 0), sh[:8, :])
        sh = jnp.concatenate([head, sh[8:, :]], axis=0)
        acc = acc + sh * cw_ref[conv_k - 1 - back:conv_k - back, :]
    xc = _silu(acc)
    xc_ref[...] = xc.astype(xc_ref.dtype)
    xcb = xc.astype(BF16)
    scale = HEAD_DIM ** -0.5
    w_mix = n_heads * HEAD_DIM
    g = jnp.zeros(gates_ref.shape, F32) + bif_ref[...]
    for h in range(n_heads):
        qh = _dot(xcb[:, _hsl(h)], wq_ref[h])
        kh = _dot(xcb[:, _hsl(h)], wk_ref[h])
        q_ref[:, _hsl(h)] = qh.astype(q_ref.dtype)
        k_ref[:, _hsl(h)] = (kh * scale).astype(k_ref.dtype)
        g = g + _dot_nt(wif_ref[:, _hsl(h)], qh.astype(BF16))
        g = g + _dot_nt(wif_ref[:, w_mix + h * HEAD_DIM:w_mix + (h + 1) * HEAD_DIM], kh.astype(BF16))
    g = g + _dot_nt(wif_ref[:, 2 * w_mix:], v_ref[...])
    gates_ref[...] = g


def _mlstm_pre(proj, conv_w, conv_b, w_q, w_k, w_if_t, b_if, layer, w_mix):
    s = proj.shape[0]
    h = w_mix // HEAD_DIM
    bm = min(512, s)
    conv_k = conv_w.shape[1]
    nl = conv_w.shape[0]
    lay3 = lambda i: (layer, 0, 0)
    return pl.pallas_call(
        functools.partial(_mlstm_pre_kernel, n_heads=h, conv_k=conv_k),
        grid=(s // bm,),
        in_specs=[pl.BlockSpec((bm, w_mix), lambda i: (i, 6)),
                  pl.BlockSpec((16, w_mix), lambda i: (jnp.maximum(i * (bm // 16) - 1, 0), 6)),
                  pl.BlockSpec((bm, w_mix), lambda i: (i, 7)),
                  pl.BlockSpec((None, conv_k, w_mix), lay3),
                  pl.BlockSpec((None, 1, w_mix), lay3),
                  pl.BlockSpec((None, h, HEAD_DIM, HEAD_DIM), lambda i: (layer, 0, 0, 0)),
                  pl.BlockSpec((None, h, HEAD_DIM, HEAD_DIM), lambda i: (layer, 0, 0, 0)),
                  pl.BlockSpec((None, 2 * h, 3 * w_mix), lay3),
                  pl.BlockSpec((None, 2 * h, 1), lay3)],
        out_specs=[pl.BlockSpec((bm, w_mix), lambda i: (i, 0)),
                   pl.BlockSpec((bm, w_mix), lambda i: (i, 0)),
                   pl.BlockSpec((bm, w_mix), lambda i: (i, 0)),
                   pl.BlockSpec((2 * h, bm), lambda i: (0, i))],
        out_shape=[jax.ShapeDtypeStruct((s, w_mix), BF16),
                   jax.ShapeDtypeStruct((s, w_mix), BF16),
                   jax.ShapeDtypeStruct((s, w_mix), BF16),
                   jax.ShapeDtypeStruct((2 * h, s), F32)],
        compiler_params=_params("parallel"),
        name="mlstm_pre",
    )(proj, proj, proj, conv_w, conv_b.reshape(nl, 1, w_mix), w_q, w_k, w_if_t, b_if.reshape(nl, 2 * h, 1))


def _mlstm_kernel(*refs, rows, hps, factors):
    ((q_ref, k_ref, v_ref, li_ref, gf_ref, xc_ref, z_ref, skip_ref, gain_ref), side_in, (o_ref,), side_out,
     (c_ref, m_ref)) = _split_refs(refs, 9, 1, len(factors))
    _cast_sides(side_in, side_out, factors)

    @pl.when(pl.program_id(1) == 0)
    def _():
        c_ref[...] = jnp.zeros_like(c_ref)
        m_ref[...] = jnp.zeros_like(m_ref)

    L = REC_CHUNK
    r_i = lax.broadcasted_iota(jnp.int32, (L, L), 0)
    c_i = lax.broadcasted_iota(jnp.int32, (L, L), 1)
    upto = (r_i <= c_i).astype(BF16)
    causal = c_i <= r_i

    def chunk(c, carry):
        rs = pl.ds(pl.multiple_of(c * L, L), L)
        heads = range(hps)

        def stage_gates(hh):
            li = li_ref[0, hh, :, rs]
            lf = _log_sigmoid(gf_ref[0, hh, :, rs])
            hi, mid, lo = _split3(jnp.broadcast_to(lf, (8, L)))
            cum = (_dot(hi, upto) + _dot(mid, upto) + _dot(lo, upto))[0:1, :]
            return li, cum

        def stage_weights(hh, li, cum):
            q = q_ref[rs, _hsl(hh)]
            k = k_ref[rs, _hsl(hh)]
            cum_t = jnp.broadcast_to(cum, (L, L)).T
            m_old = m_ref[hh]
            dmat = jnp.where(causal, cum_t - cum + li, NEG_BIG)
            inter = cum_t[:, 0:1] + m_old
            m_t = jnp.maximum(inter, jnp.max(dmat, axis=-1, keepdims=True))
            w = jnp.exp(dmat - m_t) * _dot_nt(q, k)
            return w, jnp.exp(inter - m_t), m_t

        def v_ones(hh):
            return jnp.concatenate([v_ref[rs, _hsl(hh)], jnp.ones((L, HEAD_DIM), BF16)], axis=1)

        def stage_output(hh, w, carry_w, m_t):
            q = q_ref[rs, _hsl(hh)]
            both = _dot(w.astype(BF16), v_ones(hh)) + carry_w * _dot(q, c_ref[hh].astype(BF16))
            num, den = both[:, :HEAD_DIM], both[:, HEAD_DIM:]
            h_out = num / jnp.maximum(jnp.abs(den), jnp.exp(-m_t))
            ms = jnp.mean(h_out * h_out, axis=-1, keepdims=True)
            hn = h_out * lax.rsqrt(ms + EPS) * gain_ref[...]
            y = ((hn + skip_ref[:, _hsl(hh)] * xc_ref[rs, _hsl(hh)].astype(F32))
                 * _silu(z_ref[rs, _hsl(hh)].astype(F32)))
            o_ref[rs, _hsl(hh)] = y.astype(o_ref.dtype)

        def stage_state(hh, li, cum):
            k = k_ref[rs, _hsl(hh)]
            m_old = m_ref[hh]
            last = cum[:, L - 1:L]
            g_row = last - cum + li
            m_new = jnp.maximum(last + m_old, jnp.max(g_row, axis=-1, keepdims=True))
            wk = jnp.exp(g_row - m_new)
            kw = k.astype(F32) * jnp.broadcast_to(wk, (L, L)).T
            decay = jnp.exp(last + m_old - m_new)
            c_ref[hh] = decay * c_ref[hh] + _dot(kw.T.astype(BF16), v_ones(hh))
            m_ref[hh] = m_new

        gates = [stage_gates(hh) for hh in heads]
        weights = [stage_weights(hh, *gates[hh]) for hh in heads]
        for hh in heads:
            stage_output(hh, *weights[hh])
        for hh in heads:
            stage_state(hh, *gates[hh])
        return carry

    lax.fori_loop(0, rows // L, chunk, 0)


def _mlstm(q_c, k_c, proj, gates, xc, skip, out_gain, layer, w_mix, sides=()):
    s = q_c.shape[0]
    h = w_mix // HEAD_DIM
    hps = min(MLSTM_HEADS_PER_STEP, h)
    hg = h // hps
    wide = hps * HEAD_DIM
    rows = min(REC_ROWS, s)
    nl = skip.shape[0]
    gates4 = gates.reshape(2, h, 1, s)
    hd = lambda: pl.BlockSpec((rows, wide), lambda g, c: (c, g))
    grid = (hg, s // rows)
    side_in, side_out, side_shapes = _side_specs(sides, grid)
    return pl.pallas_call(
        functools.partial(_mlstm_kernel, rows=rows, hps=hps, factors=tuple(f for _, _, f in sides)),
        grid=grid,
        in_specs=[hd(), hd(),
                  pl.BlockSpec((rows, wide), lambda g, c: (c, 7 * hg + g)),
                  pl.BlockSpec((1, hps, 1, rows), lambda g, c: (0, g, 0, c)),
                  pl.BlockSpec((1, hps, 1, rows), lambda g, c: (1, g, 0, c)),
                  hd(),
                  pl.BlockSpec((rows, wide), lambda g, c: (c, 8 * hg + g)),
                  pl.BlockSpec((None, 1, wide), lambda g, c: (layer, 0, g)),
                  pl.BlockSpec((None, 1, HEAD_DIM), lambda g, c: (layer, 0, 0))] + side_in,
        out_specs=[hd()] + side_out,
        out_shape=[jax.ShapeDtypeStruct((s, w_mix), BF16)] + side_shapes,
        scratch_shapes=[pltpu.VMEM((hps, HEAD_DIM, 2 * HEAD_DIM), F32),
                        pltpu.VMEM((hps, 1, 1), F32)],
        compiler_params=_params("arbitrary", "arbitrary"),
        name="mlstm",
    )(q_c, k_c, proj, gates4, gates4, xc, proj, skip.reshape(nl, 1, w_mix),
      out_gain.reshape(nl, 1, HEAD_DIM), *[src for src, _, _ in sides])


def _merge_kernel(xg_ref, ssq_ref, wg_ref, ya_ref, yb_ref, yc_ref, wa_ref, wb_ref, wc_ref, o_ref):
    xg = xg_ref[...]
    r = _row_factor(ssq_ref, xg.shape[1])
    out = None
    for b, (y_ref, w_ref) in enumerate(((ya_ref, wa_ref), (yb_ref, wb_ref), (yc_ref, wc_ref))):
        gate = jax.nn.sigmoid(r * _dot(xg, wg_ref[b]))
        term = gate * _dot(y_ref[...], w_ref[...])
        out = term if out is None else out + term
    o_ref[...] = out.astype(o_ref.dtype)


def _merge(xg, ssq, w_gate, ys, w_branches):
    s, d = xg.shape
    w_mix = ys[0].shape[1]
    bm, bn = min(ROW_TILE, s), min(MERGE_COL_TILE, d)
    yspec = pl.BlockSpec((bm, w_mix), lambda i, j: (i, 0))
    wspec = pl.BlockSpec((w_mix, bn), lambda i, j: (0, j))
    return pl.pallas_call(
        _merge_kernel,
        grid=(s // bm, d // bn),
        in_specs=[pl.BlockSpec((bm, d), lambda i, j: (i, 0)),
                  pl.BlockSpec((bm, SSQ_LANES), lambda i, j: (i, 0)),
                  pl.BlockSpec((3, d, bn), lambda i, j: (0, 0, j)),
                  yspec, yspec, yspec, wspec, wspec, wspec],
        out_specs=pl.BlockSpec((bm, bn), lambda i, j: (i, j)),
        out_shape=jax.ShapeDtypeStruct((s, d), BF16),
        compiler_params=_params("parallel", "arbitrary"),
        name="merge",
    )(xg, ssq, w_gate, *ys, *w_branches)


def kernel(x, ffn1_norm, ffn1_w_gate, ffn1_w_up, ffn1_w_down, mix_norm, w_in, sb_q_gain, sb_k_gain,
           hg_lb_logits, hg_out_gain, ml_conv_w, ml_conv_b, ml_w_q, ml_w_k, ml_w_if, ml_b_if,
           ml_out_gain, ml_skip, w_merge_gate, w_branch_a, w_branch_b, w_branch_c, w_out,
           ffn2_norm, ffn2_w_gate, ffn2_w_up, ffn2_w_down):
    batch, seq, d_model = x.shape
    depth = w_in.shape[0]
    w_mix = w_branch_a.shape[1]
    scale = HEAD_DIM ** -0.5
    bf = lambda a: a.astype(BF16)

    gates_f32 = w_merge_gate.reshape(depth, 3 * d_model, d_model)
    first = (bf(ffn1_w_gate[0]), bf(ffn1_w_up[0]), bf(0.5 * ffn1_w_down[0]), bf(w_in[0]))
    wq_b, wk_b, wif_t = bf(ml_w_q), bf(ml_w_k), bf(jnp.swapaxes(ml_w_if, 1, 2))
    qk_gains = jnp.stack([sb_q_gain * (-scale * LOG2_E), sb_k_gain], axis=1).reshape(depth, 2, 1, HEAD_DIM)

    outs = []
    for bi in range(batch):
        xs = x[bi]
        f1g, f1u, f1d, w_in_l = first
        xg, ssq = _norm_prep(xs, ffn1_norm, 0)
        for l in range(depth):
            more = l + 1 < depth
            hid = _ffn_up(xg, ssq, f1g, f1u)
            xs, xg, ssq = _mm_residual(hid, f1d, xs, mix_norm, l)

            proj, f_pre = _in_proj(xg, ssq, w_in_l, w_mix, qk_gains, l)
            y_a, w_out_l, wb_a, wb_b, wb_c, *nxt_in = _sb_attention(
                proj, w_mix,
                [(w_out, l, 1.0), (w_branch_a, l, 1.0), (w_branch_b, l, 1.0), (w_branch_c, l, 1.0)]
                + ([(w_in, l + 1, 1.0)] if more else []))
            y_b, gates_l, f2g, f2u = _hgrn2(
                f_pre, proj, hg_lb_logits, hg_out_gain, w_mix, l,
                [(gates_f32, l, 1.0), (ffn2_w_gate, l, 1.0), (ffn2_w_up, l, 1.0)])
            xc, q_c, k_c, gates = _mlstm_pre(proj, ml_conv_w, ml_conv_b, wq_b, wk_b, wif_t, ml_b_if, l, w_mix)
            y_c, f2d, *nxt_ffn = _mlstm(
                q_c, k_c, proj, gates, xc, ml_skip, ml_out_gain, l, w_mix,
                [(ffn2_w_down, l, 0.5)]
                + ([(ffn1_w_gate, l + 1, 1.0), (ffn1_w_up, l + 1, 1.0), (ffn1_w_down, l + 1, 0.5)] if more else []))
            merged = _merge(xg, ssq, gates_l.reshape(3, d_model, d_model), (y_a, y_b, y_c), (wb_a, wb_b, wb_c))
            xs, xg, ssq = _mm_residual(merged, w_out_l, xs, ffn2_norm, l)

            hid = _ffn_up(xg, ssq, f2g, f2u)
            if more:
                xs, xg, ssq = _mm_residual(hid, f2d, xs, ffn1_norm, l + 1)
                (f1g, f1u, f1d), (w_in_l,) = nxt_ffn, nxt_in
            else:
                xs = _mm_residual(hid, f2d, xs)
        outs.append(xs)
    return outs[0][None] if batch == 1 else jnp.stack(outs)
```

```python
import functools

import jax
import jax.numpy as jnp
from jax import lax
from jax.experimental import pallas as pl
from jax.experimental.pallas import tpu as pltpu

F32 = jnp.float32
BF16 = jnp.bfloat16

HEAD_DIM = 128
EPS = 1e-6
NEG_BIG = -1e30
LB_FLOOR = 1e-30
LOG2_E = 1.4426950408889634
STICK_FLOOR_LOG2 = -160.0

V7X_VMEM_LIMIT_BYTES = 56 * 1024 * 1024
ROW_TILE = 1024
COL_TILE = 512
PROJ_COL_TILE = 1024
RES_COL_TILE = 512
MERGE_COL_TILE = 256
SSQ_LANES = 128
ATT_TILE = 256
ATT_SLAB = 16
REC_CHUNK = 128
REC_ROWS = 512
SUB = 16
HGRN_HEADS_PER_STEP = 4
MLSTM_HEADS_PER_STEP = 4
ATT_HEADS_PER_STEP = 4


def _params(*sem):
    return pltpu.CompilerParams(dimension_semantics=sem, vmem_limit_bytes=V7X_VMEM_LIMIT_BYTES)


def _dot(a, b):
    return jnp.dot(a, b, preferred_element_type=F32)


def _dot_nt(a, b):
    return lax.dot_general(a, b, (((1,), (1,)), ((), ())), preferred_element_type=F32)


def _split3(x):
    hi = x.astype(BF16)
    r1 = x - hi.astype(F32)
    mid = r1.astype(BF16)
    lo = (r1 - mid.astype(F32)).astype(BF16)
    return hi, mid, lo


def _softplus(z):
    return jnp.maximum(z, 0.0) + jnp.log(1.0 + jnp.exp(-jnp.abs(z)))


def _log_sigmoid(x):
    return jnp.minimum(x, 0.0) - jnp.log(1.0 + jnp.exp(-jnp.abs(x)))


def _silu(x):
    return x * jax.nn.sigmoid(x)


def _hsl(hh):
    return slice(hh * HEAD_DIM, (hh + 1) * HEAD_DIM)


def _side_specs(sides, grid):
    steps = 1
    for n in grid:
        steps *= n

    def chunk(*g):
        idx = g[0]
        for a, n in zip(g[1:], grid[1:]):
            idx = idx * n + a
        return idx

    in_specs, out_specs, out_shapes = [], [], []
    for src, layer, _ in sides:
        _, r, c = src.shape
        rb = r // steps
        assert rb * steps == r and rb % 16 == 0, (src.shape, steps)
        in_specs.append(pl.BlockSpec((None, rb, c), lambda *g, layer=layer: (layer, chunk(*g), 0)))
        out_specs.append(pl.BlockSpec((rb, c), lambda *g: (chunk(*g), 0)))
        out_shapes.append(jax.ShapeDtypeStruct((r, c), BF16))
    return in_specs, out_specs, out_shapes


def _cast_sides(in_refs, out_refs, factors):
    for i_ref, o_ref, f in zip(in_refs, out_refs, factors):
        v = i_ref[...]
        o_ref[...] = (v if f == 1.0 else v * f).astype(o_ref.dtype)


def _split_refs(refs, n_in, n_out, n_side):
    a, b, c, d = n_in, n_in + n_side, n_in + n_side + n_out, n_in + 2 * n_side + n_out
    return refs[:a], refs[a:b], refs[b:c], refs[c:d], refs[d:]


def _row_factor(ssq_ref, d):
    return lax.rsqrt(ssq_ref[:, 0:1] * (1.0 / d) + EPS)


def _norm_prep_kernel(x_ref, g_ref, xg_ref, ssq_ref):
    x = x_ref[...]
    xg_ref[...] = (x * g_ref[...]).astype(xg_ref.dtype)
    ssq_ref[...] = jnp.broadcast_to(jnp.sum(x * x, axis=-1, keepdims=True), ssq_ref.shape)


def _norm_prep(x, gains, layer):
    s, d = x.shape
    bm = min(256, s)
    return pl.pallas_call(
        _norm_prep_kernel,
        grid=(s // bm,),
        in_specs=[pl.BlockSpec((bm, d), lambda i: (i, 0)),
                  pl.BlockSpec((None, 1, d), lambda i: (layer, 0, 0))],
        out_specs=[pl.BlockSpec((bm, d), lambda i: (i, 0)),
                   pl.BlockSpec((bm, SSQ_LANES), lambda i: (i, 0))],
        out_shape=[jax.ShapeDtypeStruct((s, d), BF16),
                   jax.ShapeDtypeStruct((s, SSQ_LANES), F32)],
        compiler_params=_params("parallel"),
        name="norm_prep",
    )(x, gains.reshape(gains.shape[0], 1, d))


def _ffn_up_kernel(xg_ref, ssq_ref, wg_ref, wu_ref, o_ref):
    xg = xg_ref[...]
    r = _row_factor(ssq_ref, xg.shape[1])
    g = r * _dot(xg, wg_ref[...])
    u = r * _dot(xg, wu_ref[...])
    o_ref[...] = (_silu(g) * u).astype(o_ref.dtype)


def _ffn_up(xg, ssq, wg, wu):
    s, d = xg.shape
    f = wg.shape[1]
    bm, bn = min(ROW_TILE, s), min(COL_TILE, f)
    wspec = pl.BlockSpec((d, bn), lambda i, j: (0, j))
    return pl.pallas_call(
        _ffn_up_kernel,
        grid=(s // bm, f // bn),
        in_specs=[pl.BlockSpec((bm, d), lambda i, j: (i, 0)),
                  pl.BlockSpec((bm, SSQ_LANES), lambda i, j: (i, 0)), wspec, wspec],
        out_specs=pl.BlockSpec((bm, bn), lambda i, j: (i, j)),
        out_shape=jax.ShapeDtypeStruct((s, f), BF16),
        compiler_params=_params("parallel", "arbitrary"),
        name="ffn_up",
    )(xg, ssq, wg, wu)


def _mm_res_kernel(a_ref, w_ref, x_ref, o_ref):
    o_ref[...] = x_ref[...] + _dot(a_ref[...], w_ref[...])


def _mm_res_norm_kernel(a_ref, w_ref, x_ref, g_ref, o_ref, xg_ref, ssq_ref):
    y = x_ref[...] + _dot(a_ref[...], w_ref[...])
    o_ref[...] = y
    xg_ref[...] = (y * g_ref[...]).astype(xg_ref.dtype)
    part = jnp.broadcast_to(jnp.sum(y * y, axis=-1, keepdims=True), ssq_ref.shape)

    @pl.when(pl.program_id(1) == 0)
    def _():
        ssq_ref[...] = part

    @pl.when(pl.program_id(1) > 0)
    def _():
        ssq_ref[...] += part


def _mm_residual(a, w, x, next_gain=None, next_layer=0):
    s, k = a.shape
    n = w.shape[1]
    bm, bn = min(ROW_TILE, s), min(RES_COL_TILE, n)
    in_specs = [pl.BlockSpec((bm, k), lambda i, j: (i, 0)),
                pl.BlockSpec((k, bn), lambda i, j: (0, j)),
                pl.BlockSpec((bm, bn), lambda i, j: (i, j))]
    tile = pl.BlockSpec((bm, bn), lambda i, j: (i, j))
    if next_gain is None:
        return pl.pallas_call(
            _mm_res_kernel,
            grid=(s // bm, n // bn),
            in_specs=in_specs,
            out_specs=tile,
            out_shape=jax.ShapeDtypeStruct((s, n), F32),
            compiler_params=_params("parallel", "arbitrary"),
            name="mm_residual",
        )(a, w, x)
    return pl.pallas_call(
        _mm_res_norm_kernel,
        grid=(s // bm, n // bn),
        in_specs=in_specs + [pl.BlockSpec((None, 1, bn), lambda i, j: (next_layer, 0, j))],
        out_specs=[tile, tile, pl.BlockSpec((bm, SSQ_LANES), lambda i, j: (i, 0))],
        out_shape=[jax.ShapeDtypeStruct((s, n), F32),
                   jax.ShapeDtypeStruct((s, n), BF16),
                   jax.ShapeDtypeStruct((s, SSQ_LANES), F32)],
        compiler_params=_params("parallel", "arbitrary"),
        name="mm_residual_norm",
    )(a, w, x, next_gain.reshape(next_gain.shape[0], 1, n))


def _mm_kernel(xg_ref, ssq_ref, w_ref, gain_ref, o_ref, *, qk_blocks):
    xg = xg_ref[...]
    y = _row_factor(ssq_ref, xg.shape[1]) * _dot(xg, w_ref[...])
    if qk_blocks == 0:
        o_ref[...] = y.astype(o_ref.dtype)
        return
    j = pl.program_id(1)

    @pl.when(j >= qk_blocks)
    def _():
        o_ref[...] = y.astype(o_ref.dtype)

    @pl.when(j < qk_blocks)
    def _():
        g = gain_ref[...]
        for h in range(y.shape[1] // HEAD_DIM):
            yh = y[:, _hsl(h)]
            ms = jnp.mean(yh * yh, axis=-1, keepdims=True)
            o_ref[:, _hsl(h)] = (yh * lax.rsqrt(ms + EPS) * g).astype(o_ref.dtype)


def _mm(xg, ssq, w, n_out, bn, col_block_of, out_dtype, qk_gains, layer, qk_blocks, nblk):
    s, k = xg.shape
    bm = min(ROW_TILE, s)
    return pl.pallas_call(
        functools.partial(_mm_kernel, qk_blocks=qk_blocks),
        grid=(s // bm, n_out // bn),
        in_specs=[pl.BlockSpec((bm, k), lambda i, j: (i, 0)),
                  pl.BlockSpec((bm, SSQ_LANES), lambda i, j: (i, 0)),
                  pl.BlockSpec((k, bn), lambda i, j: (0, col_block_of(j))),
                  pl.BlockSpec((None, None, 1, HEAD_DIM),
                               lambda i, j: (layer, jnp.minimum(j // nblk, 1), 0, 0))],
        out_specs=pl.BlockSpec((bm, bn), lambda i, j: (i, j)),
        out_shape=jax.ShapeDtypeStruct((s, n_out), out_dtype),
        compiler_params=_params("parallel", "arbitrary"),
        name="in_proj",
    )(xg, ssq, w, qk_gains)


def _in_proj(xg, ssq, w_in, w_mix, qk_gains, layer):
    bn = min(PROJ_COL_TILE, w_mix)
    nblk = w_mix // bn
    proj = _mm(xg, ssq, w_in, 9 * w_mix, bn, lambda j: j + jnp.where(j >= 3 * nblk, nblk, 0), BF16,
               qk_gains, layer, 2 * nblk, nblk)
    f_pre = _mm(xg, ssq, w_in, w_mix, bn, lambda j: j + 3 * nblk, F32, qk_gains, layer, 0, nblk)
    return proj, f_pre


def _sb_kernel(*refs, tile, hps, factors):
    (q_ref, k_ref, v_ref), side_in, (o_ref,), side_out, (acc_ref, run_ref, nz_ref) = _split_refs(
        refs, 3, 1, len(factors))
    _cast_sides(side_in, side_out, factors)
    qi = pl.program_id(1)
    row = lax.broadcasted_iota(jnp.int32, (tile, tile), 0)
    col = lax.broadcasted_iota(jnp.int32, (tile, tile), 1)
    later = (row > col).astype(BF16)
    causal = col < row
    heads = range(hps)

    acc_ref[...] = jnp.zeros_like(acc_ref)
    run_ref[...] = jnp.zeros_like(run_ref)

    def logits(kb, hh):
        off = pl.multiple_of(kb * tile, tile)
        nz_ref[hh] = _dot_nt(q_ref[:, _hsl(hh)], k_ref[pl.ds(off, tile), _hsl(hh)])

    def block(kb, masked, kb_next):
        off = pl.multiple_of(kb * tile, tile)

        def stage_softplus(hh):
            nz = nz_ref[hh]
            neg_abs = lax.bitcast_convert_type(
                lax.bitcast_convert_type(nz, jnp.uint32) | jnp.uint32(0x80000000), F32)
            log_1m = jnp.minimum(nz, 0.0) - jnp.log2(1.0 + jnp.exp2(neg_abs))
            log_beta = log_1m - nz
            if masked:
                log_1m = jnp.where(causal, log_1m, 0.0)
            between = _dot(log_1m.astype(BF16), later)
            return log_beta, between, jnp.sum(log_1m, axis=-1, keepdims=True)

        def stage_out(hh, log_beta, between, total):
            att = jnp.exp2(log_beta + between)
            if masked:
                att = jnp.where(causal, att, 0.0)
            run = run_ref[hh]
            acc_ref[hh] += jnp.exp2(run) * _dot(att.astype(BF16), v_ref[pl.ds(off, tile), _hsl(hh)])
            run_ref[hh] = run + total

        mids = []
        for hh in heads:
            mids.append(stage_softplus(hh))
            logits(kb_next, hh)
        for hh in heads:
            stage_out(hh, *mids[hh])

    for hh in heads:
        logits(qi, hh)
    block(qi, True, jnp.maximum(qi - 1, 0))

    def stick_left():
        top = run_ref[0]
        for hh in range(1, hps):
            top = jnp.maximum(top, run_ref[hh])
        return jnp.max(top) > STICK_FLOOR_LOG2

    def body(state):
        it, _ = state
        kb = qi - 1 - it
        block(kb, False, jnp.maximum(kb - 1, 0))
        return it + 1, stick_left()

    lax.while_loop(lambda st: jnp.logical_and(st[0] < qi, st[1]), body, (jnp.int32(0), stick_left()))
    for hh in heads:
        o_ref[:, _hsl(hh)] = acc_ref[hh].astype(o_ref.dtype)


def _sb_attention(proj, w_mix, sides=()):
    s = proj.shape[0]
    h = w_mix // HEAD_DIM
    hps = min(ATT_HEADS_PER_STEP, h)
    hg = h // hps
    wide = hps * HEAD_DIM
    tile = min(ATT_TILE, s)
    grid = (hg, s // tile)
    side_in, side_out, side_shapes = _side_specs(sides, grid)
    return pl.pallas_call(
        functools.partial(_sb_kernel, tile=tile, hps=hps, factors=tuple(f for _, _, f in sides)),
        grid=grid,
        in_specs=[pl.BlockSpec((tile, wide), lambda g, i: (i, g)),
                  pl.BlockSpec((s, wide), lambda g, i: (0, hg + g)),
                  pl.BlockSpec((s, wide), lambda g, i: (0, 2 * hg + g))] + side_in,
        out_specs=[pl.BlockSpec((tile, wide), lambda g, i: (i, g))] + side_out,
        out_shape=[jax.ShapeDtypeStruct((s, w_mix), BF16)] + side_shapes,
        scratch_shapes=[pltpu.VMEM((hps, tile, HEAD_DIM), F32),
                        pltpu.VMEM((hps, tile, 1), F32),
                        pltpu.VMEM((hps, tile, tile), F32)],
        compiler_params=_params("arbitrary", "arbitrary"),
        name="sb_attention",
    )(proj, proj, proj, *[src for src, _, _ in sides])


def _hgrn2_kernel(*refs, layer, rows, hps, factors):
    (f_ref, v_ref, q_ref, g_ref, lbl_ref, gain_ref), side_in, (o_ref,), side_out, (st_ref,) = _split_refs(
        refs, 6, 1, len(factors))
    _cast_sides(side_in, side_out, factors)

    @pl.when(pl.program_id(1) == 0)
    def _():
        st_ref[...] = jnp.zeros_like(st_ref)

    logits = lbl_ref[...]
    e = jnp.exp(logits - jnp.max(logits, axis=0, keepdims=True))
    p = e / jnp.sum(e, axis=0, keepdims=True)
    lb_all = jnp.zeros((1, hps * HEAD_DIM), F32)
    for i in range(1, layer + 1):
        lb_all = lb_all + p[i:i + 1, :]

    L = REC_CHUNK
    r_i = lax.broadcasted_iota(jnp.int32, (L, L), 0)
    c_i = lax.broadcasted_iota(jnp.int32, (L, L), 1)
    tri_incl = (c_i <= r_i).astype(BF16)
    half_row = lax.broadcasted_iota(jnp.int32, (SUB // 2, HEAD_DIM), 0)

    def chunk(c, carry):
        rs = pl.ds(pl.multiple_of(c * L, L), L)

        def stage_gates(hh):
            lb = lb_all[:, _hsl(hh)]
            log_lb = jnp.log(jnp.maximum(lb, LB_FLOOR))
            x = f_ref[rs, _hsl(hh)]
            b = jnp.log1p(-lb) + _log_sigmoid(x)
            lf = jnp.maximum(log_lb, b) + jnp.log(1.0 + jnp.exp(-jnp.abs(log_lb - b)))
            kk = (1.0 - lb) * jax.nn.sigmoid(-x)
            qq = _silu(q_ref[rs, _hsl(hh)].astype(F32))
            hi, mid, lo = _split3(lf)
            cum = (_dot(tri_incl, hi) + _dot(tri_incl, mid) + _dot(tri_incl, lo)) * LOG2_E
            return kk, qq, cum

        def stage_mix(hh, kk, qq, cum):
            v = v_ref[rs, _hsl(hh)]
            vf = v.astype(F32)
            last = cum[L - 1:L, :]

            st = st_ref[hh]
            out = _dot_nt((qq * jnp.exp2(cum)).astype(BF16), st.astype(BF16))

            a_rows = [jnp.zeros((SUB, L), F32)]
            for i in range(1, L // SUB):
                lo_r, hi_r = i * SUB, (i + 1) * SUB
                anchor = cum[lo_r - 1:lo_r, :]
                qt = qq[lo_r:hi_r, :] * jnp.exp2(cum[lo_r:hi_r, :] - anchor)
                kt = kk[:lo_r, :] * jnp.exp2(anchor - cum[:lo_r, :])
                kt = jnp.concatenate([kt, jnp.zeros((L - lo_r, HEAD_DIM), F32)], axis=0)
                a_rows.append(_dot_nt(qt.astype(BF16), kt.astype(BF16)))
            a_off = jnp.concatenate(a_rows, axis=0)
            out = out + _dot(a_off.astype(BF16), v)

            diag = []
            for i in range(L // SUB):
                base = i * SUB
                halves = []
                for r0 in (0, SUB // 2):
                    rows_r = slice(base + r0, base + r0 + SUB // 2)
                    cb, qb = cum[rows_r, :], qq[rows_r, :]
                    o_half = jnp.zeros((SUB // 2, HEAD_DIM), F32)
                    for s_ in range(r0 + SUB // 2):
                        srow = slice(base + s_, base + s_ + 1)
                        diff = cb - cum[srow, :]
                        if s_ > r0:
                            diff = jnp.where(half_row >= s_ - r0, diff, NEG_BIG)
                        a_col = jnp.sum(qb * jnp.exp2(diff) * kk[srow, :], axis=-1, keepdims=True)
                        o_half = o_half + a_col * vf[srow, :]
                    halves.append(o_half)
                diag.extend(halves)
            out = out + jnp.concatenate(diag, axis=0)

            kt = kk * jnp.exp2(last - cum)
            st_ref[hh] = jnp.exp2(last) * st + _dot(vf.T.astype(BF16), kt.astype(BF16))

            ms = jnp.mean(out * out, axis=-1, keepdims=True)
            y = out * lax.rsqrt(ms + EPS) * gain_ref[...]
            o_ref[rs, _hsl(hh)] = (y * _silu(g_ref[rs, _hsl(hh)].astype(F32))).astype(o_ref.dtype)

        gated = [stage_gates(hh) for hh in range(hps)]
        for hh in range(hps):
            stage_mix(hh, *gated[hh])
        return carry

    lax.fori_loop(0, rows // L, chunk, 0)


def _hgrn2(f_pre, proj, lb_logits, out_gain, w_mix, layer, sides=()):
    s = f_pre.shape[0]
    h = w_mix // HEAD_DIM
    hps = min(HGRN_HEADS_PER_STEP, h)
    hg = h // hps
    wide = hps * HEAD_DIM
    rows = min(REC_ROWS, s)
    depth = lb_logits.shape[0]
    blk = lambda grp: pl.BlockSpec((rows, wide), lambda g, c: (c, grp * hg + g))
    grid = (hg, s // rows)
    side_in, side_out, side_shapes = _side_specs(sides, grid)
    return pl.pallas_call(
        functools.partial(_hgrn2_kernel, layer=layer, rows=rows, hps=hps,
                          factors=tuple(f for _, _, f in sides)),
        grid=grid,
        in_specs=[pl.BlockSpec((rows, wide), lambda g, c: (c, g)),
                  blk(3), blk(4), blk(5),
                  pl.BlockSpec((depth, wide), lambda g, c: (0, g)),
                  pl.BlockSpec((None, 1, HEAD_DIM), lambda g, c: (layer, 0, 0))] + side_in,
        out_specs=[pl.BlockSpec((rows, wide), lambda g, c: (c, g))] + side_out,
        out_shape=[jax.ShapeDtypeStruct((s, w_mix), BF16)] + side_shapes,
        scratch_shapes=[pltpu.VMEM((hps, HEAD_DIM, HEAD_DIM), F32)],
        compiler_params=_params("arbitrary", "arbitrary"),
        name="hgrn2",
    )(f_pre, proj, proj, proj, lb_logits, out_gain.reshape(out_gain.shape[0], 1, HEAD_DIM),
      *[src for src, _, _ in sides])


def _mlstm_pre_kernel(x_ref, prev_ref, v_ref, cw_ref, cb_ref, wq_ref, wk_ref, wif_ref, bif_ref,
                      xc_ref, q_ref, k_ref, gates_ref, *, n_heads, conv_k):
    i = pl.program_id(0)
    x = x_ref[...].astype(F32)
    prev = prev_ref[8:16, :].astype(F32)
    prev = jnp.where(i > 0, prev, 0.0)
    row8 = lax.broadcasted_iota(jnp.int32, prev.shape, 0)
    acc = x * cw_ref[conv_k - 1:conv_k, :] + cb_ref[...]
    for back in range(1, conv_k):
        sh = pltpu.roll(x, back, 0)
        head = jnp.where(row8 < back, pltpu.roll(prev, back, 0), sh[:8, :])
        sh = jnp.concatenate([head, sh[8:, :]], axis=0)
        acc = acc + sh * cw_ref[conv_k - 1 - back:conv_k - back, :]
    xc = _silu(acc)
    xc_ref[...] = xc.astype(xc_ref.dtype)
    xcb = xc.astype(BF16)
    scale = HEAD_DIM ** -0.5
    w_mix = n_heads * HEAD_DIM
    g = jnp.zeros(gates_ref.shape, F32) + bif_ref[...]
    for h in range(n_heads):
        qh = _dot(xcb[:, _hsl(h)], wq_ref[h])
        kh = _dot(xcb[:, _hsl(h)], wk_ref[h])
        q_ref[:, _hsl(h)] = qh.astype(q_ref.dtype)
        k_ref[:, _hsl(h)] = (kh * scale).astype(k_ref.dtype)
        g = g + _dot_nt(wif_ref[:, _hsl(h)], qh.astype(BF16))
        g = g + _dot_nt(wif_ref[:, w_mix + h * HEAD_DIM:w_mix + (h + 1) * HEAD_DIM], kh.astype(BF16))
    g = g + _dot_nt(wif_ref[:, 2 * w_mix:], v_ref[...])
    gates_ref[...] = g


def _mlstm_pre(proj, conv_w, conv_b, w_q, w_k, w_if_t, b_if, layer, w_mix):
    s = proj.shape[0]
    h = w_mix // HEAD_DIM
    bm = min(512, s)
    conv_k = conv_w.shape[1]
    nl = conv_w.shape[0]
    lay3 = lambda i: (layer, 0, 0)
    return pl.pallas_call(
        functools.partial(_mlstm_pre_kernel, n_heads=h, conv_k=conv_k),
        grid=(s // bm,),
        in_specs=[pl.BlockSpec((bm, w_mix), lambda i: (i, 6)),
                  pl.BlockSpec((16, w_mix), lambda i: (jnp.maximum(i * (bm // 16) - 1, 0), 6)),
                  pl.BlockSpec((bm, w_mix), lambda i: (i, 7)),
                  pl.BlockSpec((None, conv_k, w_mix), lay3),
                  pl.BlockSpec((None, 1, w_mix), lay3),
                  pl.BlockSpec((None, h, HEAD_DIM, HEAD_DIM), lambda i: (layer, 0, 0, 0)),
                  pl.BlockSpec((None, h, HEAD_DIM, HEAD_DIM), lambda i: (layer, 0, 0, 0)),
                  pl.BlockSpec((None, 2 * h, 3 * w_mix), lay3),
                  pl.BlockSpec((None, 2 * h, 1), lay3)],
        out_specs=[pl.BlockSpec((bm, w_mix), lambda i: (i, 0)),
                   pl.BlockSpec((bm, w_mix), lambda i: (i, 0)),
                   pl.BlockSpec((bm, w_mix), lambda i: (i, 0)),
                   pl.BlockSpec((2 * h, bm), lambda i: (0, i))],
        out_shape=[jax.ShapeDtypeStruct((s, w_mix), BF16),
                   jax.ShapeDtypeStruct((s, w_mix), BF16),
                   jax.ShapeDtypeStruct((s, w_mix), BF16),
                   jax.ShapeDtypeStruct((2 * h, s), F32)],
        compiler_params=_params("parallel"),
        name="mlstm_pre",
    )(proj, proj, proj, conv_w, conv_b.reshape(nl, 1, w_mix), w_q, w_k, w_if_t, b_if.reshape(nl, 2 * h, 1))


def _mlstm_kernel(*refs, rows, hps, factors):
    ((q_ref, k_ref, v_ref, li_ref, gf_ref, xc_ref, z_ref, skip_ref, gain_ref), side_in, (o_ref,), side_out,
     (c_ref, m_ref)) = _split_refs(refs, 9, 1, len(factors))
    _cast_sides(side_in, side_out, factors)

    @pl.when(pl.program_id(1) == 0)
    def _():
        c_ref[...] = jnp.zeros_like(c_ref)
        m_ref[...] = jnp.zeros_like(m_ref)

    L = REC_CHUNK
    r_i = lax.broadcasted_iota(jnp.int32, (L, L), 0)
    c_i = lax.broadcasted_iota(jnp.int32, (L, L), 1)
    upto = (r_i <= c_i).astype(BF16)
    causal = c_i <= r_i

    def chunk(c, carry):
        rs = pl.ds(pl.multiple_of(c * L, L), L)
        heads = range(hps)

        def stage_gates(hh):
            li = li_ref[0, hh, :, rs]
            lf = _log_sigmoid(gf_ref[0, hh, :, rs])
            hi, mid, lo = _split3(jnp.broadcast_to(lf, (8, L)))
            cum = (_dot(hi, upto) + _dot(mid, upto) + _dot(lo, upto))[0:1, :]
            return li, cum

        def stage_weights(hh, li, cum):
            q = q_ref[rs, _hsl(hh)]
            k = k_ref[rs, _hsl(hh)]
            cum_t = jnp.broadcast_to(cum, (L, L)).T
            m_old = m_ref[hh]
            dmat = jnp.where(causal, cum_t - cum + li, NEG_BIG)
            inter = cum_t[:, 0:1] + m_old
            m_t = jnp.maximum(inter, jnp.max(dmat, axis=-1, keepdims=True))
            w = jnp.exp(dmat - m_t) * _dot_nt(q, k)
            return w, jnp.exp(inter - m_t), m_t

        def v_ones(hh):
            return jnp.concatenate([v_ref[rs, _hsl(hh)], jnp.ones((L, HEAD_DIM), BF16)], axis=1)

        def stage_output(hh, w, carry_w, m_t):
            q = q_ref[rs, _hsl(hh)]
            both = _dot(w.astype(BF16), v_ones(hh)) + carry_w * _dot(q, c_ref[hh].astype(BF16))
            num, den = both[:, :HEAD_DIM], both[:, HEAD_DIM:]
            h_out = num / jnp.maximum(jnp.abs(den), jnp.exp(-m_t))
            ms = jnp.mean(h_out * h_out, axis=-1, keepdims=True)
            hn = h_out * lax.rsqrt(ms + EPS) * gain_ref[...]
            y = ((hn + skip_ref[:, _hsl(hh)] * xc_ref[rs, _hsl(hh)].astype(F32))
                 * _silu(z_ref[rs, _hsl(hh)].astype(F32)))
            o_ref[rs, _hsl(hh)] = y.astype(o_ref.dtype)

        def stage_state(hh, li, cum):
            k = k_ref[rs, _hsl(hh)]
            m_old = m_ref[hh]
            last = cum[:, L - 1:L]
            g_row = last - cum + li
            m_new = jnp.maximum(last + m_old, jnp.max(g_row, axis=-1, keepdims=True))
            wk = jnp.exp(g_row - m_new)
            kw = k.astype(F32) * jnp.broadcast_to(wk, (L, L)).T
            decay = jnp.exp(last + m_old - m_new)
            c_ref[hh] = decay * c_ref[hh] + _dot(kw.T.astype(BF16), v_ones(hh))
            m_ref[hh] = m_new

        gates = [stage_gates(hh) for hh in heads]
        weights = [stage_weights(hh, *gates[hh]) for hh in heads]
        for hh in heads:
            stage_output(hh, *weights[hh])
        for hh in heads:
            stage_state(hh, *gates[hh])
        return carry

    lax.fori_loop(0, rows // L, chunk, 0)


def _mlstm(q_c, k_c, proj, gates, xc, skip, out_gain, layer, w_mix, sides=()):
    s = q_c.shape[0]
    h = w_mix // HEAD_DIM
    hps = min(MLSTM_HEADS_PER_STEP, h)
    hg = h // hps
    wide = hps * HEAD_DIM
    rows = min(REC_ROWS, s)
    nl = skip.shape[0]
    gates4 = gates.reshape(2, h, 1, s)
    hd = lambda: pl.BlockSpec((rows, wide), lambda g, c: (c, g))
    grid = (hg, s // rows)
    side_in, side_out, side_shapes = _side_specs(sides, grid)
    return pl.pallas_call(
        functools.partial(_mlstm_kernel, rows=rows, hps=hps, factors=tuple(f for _, _, f in sides)),
        grid=grid,
        in_specs=[hd(), hd(),
                  pl.BlockSpec((rows, wide), lambda g, c: (c, 7 * hg + g)),
                  pl.BlockSpec((1, hps, 1, rows), lambda g, c: (0, g, 0, c)),
                  pl.BlockSpec((1, hps, 1, rows), lambda g, c: (1, g, 0, c)),
                  hd(),
                  pl.BlockSpec((rows, wide), lambda g, c: (c, 8 * hg + g)),
                  pl.BlockSpec((None, 1, wide), lambda g, c: (layer, 0, g)),
                  pl.BlockSpec((None, 1, HEAD_DIM), lambda g, c: (layer, 0, 0))] + side_in,
        out_specs=[hd()] + side_out,
        out_shape=[jax.ShapeDtypeStruct((s, w_mix), BF16)] + side_shapes,
        scratch_shapes=[pltpu.VMEM((hps, HEAD_DIM, 2 * HEAD_DIM), F32),
                        pltpu.VMEM((hps, 1, 1), F32)],
        compiler_params=_params("arbitrary", "arbitrary"),
        name="mlstm",
    )(q_c, k_c, proj, gates4, gates4, xc, proj, skip.reshape(nl, 1, w_mix),
      out_gain.reshape(nl, 1, HEAD_DIM), *[src for src, _, _ in sides])


def _merge_kernel(xg_ref, ssq_ref, wg_ref, ya_ref, yb_ref, yc_ref, wa_ref, wb_ref, wc_ref, o_ref):
    xg = xg_ref[...]
    r = _row_factor(ssq_ref, xg.shape[1])
    out = None
    for b, (y_ref, w_ref) in enumerate(((ya_ref, wa_ref), (yb_ref, wb_ref), (yc_ref, wc_ref))):
        gate = jax.nn.sigmoid(r * _dot(xg, wg_ref[b]))
        term = gate * _dot(y_ref[...], w_ref[...])
        out = term if out is None else out + term
    o_ref[...] = out.astype(o_ref.dtype)


def _merge(xg, ssq, w_gate, ys, w_branches):
    s, d = xg.shape
    w_mix = ys[0].shape[1]
    bm, bn = min(ROW_TILE, s), min(MERGE_COL_TILE, d)
    yspec = pl.BlockSpec((bm, w_mix), lambda i, j: (i, 0))
    wspec = pl.BlockSpec((w_mix, bn), lambda i, j: (0, j))
    return pl.pallas_call(
        _merge_kernel,
        grid=(s // bm, d // bn),
        in_specs=[pl.BlockSpec((bm, d), lambda i, j: (i, 0)),
                  pl.BlockSpec((bm, SSQ_LANES), lambda i, j: (i, 0)),
                  pl.BlockSpec((3, d, bn), lambda i, j: (0, 0, j)),
                  yspec, yspec, yspec, wspec, wspec, wspec],
        out_specs=pl.BlockSpec((bm, bn), lambda i, j: (i, j)),
        out_shape=jax.ShapeDtypeStruct((s, d), BF16),
        compiler_params=_params("parallel", "arbitrary"),
        name="merge",
    )(xg, ssq, w_gate, *ys, *w_branches)


def kernel(x, ffn1_norm, ffn1_w_gate, ffn1_w_up, ffn1_w_down, mix_norm, w_in, sb_q_gain, sb_k_gain,
           hg_lb_logits, hg_out_gain, ml_conv_w, ml_conv_b, ml_w_q, ml_w_k, ml_w_if, ml_b_if,
           ml_out_gain, ml_skip, w_merge_gate, w_branch_a, w_branch_b, w_branch_c, w_out,
           ffn2_norm, ffn2_w_gate, ffn2_w_up, ffn2_w_down):
    batch, seq, d_model = x.shape
    depth = w_in.shape[0]
    w_mix = w_branch_a.shape[1]
    scale = HEAD_DIM ** -0.5
    bf = lambda a: a.astype(BF16)

    gates_f32 = w_merge_gate.reshape(depth, 3 * d_model, d_model)
    first = (bf(ffn1_w_gate[0]), bf(ffn1_w_up[0]), bf(0.5 * ffn1_w_down[0]), bf(w_in[0]))
    wq_b, wk_b, wif_t = bf(ml_w_q), bf(ml_w_k), bf(jnp.swapaxes(ml_w_if, 1, 2))
    qk_gains = jnp.stack([sb_q_gain * (-scale * LOG2_E), sb_k_gain], axis=1).reshape(depth, 2, 1, HEAD_DIM)

    outs = []
    for bi in range(batch):
        xs = x[bi]
        f1g, f1u, f1d, w_in_l = first
        xg, ssq = _norm_prep(xs, ffn1_norm, 0)
        for l in range(depth):
            more = l + 1 < depth
            hid = _ffn_up(xg, ssq, f1g, f1u)
            xs, xg, ssq = _mm_residual(hid, f1d, xs, mix_norm, l)

            proj, f_pre = _in_proj(xg, ssq, w_in_l, w_mix, qk_gains, l)
            y_a, w_out_l, wb_a, wb_b, wb_c, *nxt_in = _sb_attention(
                proj, w_mix,
                [(w_out, l, 1.0), (w_branch_a, l, 1.0), (w_branch_b, l, 1.0), (w_branch_c, l, 1.0)]
                + ([(w_in, l + 1, 1.0)] if more else []))
            y_b, gates_l, f2g, f2u = _hgrn2(
                f_pre, proj, hg_lb_logits, hg_out_gain, w_mix, l,
                [(gates_f32, l, 1.0), (ffn2_w_gate, l, 1.0), (ffn2_w_up, l, 1.0)])
            xc, q_c, k_c, gates = _mlstm_pre(proj, ml_conv_w, ml_conv_b, wq_b, wk_b, wif_t, ml_b_if, l, w_mix)
            y_c, f2d, *nxt_ffn = _mlstm(
                q_c, k_c, proj, gates, xc, ml_skip, ml_out_gain, l, w_mix,
                [(ffn2_w_down, l, 0.5)]
                + ([(ffn1_w_gate, l + 1, 1.0), (ffn1_w_up, l + 1, 1.0), (ffn1_w_down, l + 1, 0.5)] if more else []))
            merged = _merge(xg, ssq, gates_l.reshape(3, d_model, d_model), (y_a, y_b, y_c), (wb_a, wb_b, wb_c))
            xs, xg, ssq = _mm_residual(merged, w_out_l, xs, ffn2_norm, l)

            hid = _ffn_up(xg, ssq, f2g, f2u)
            if more:
                xs, xg, ssq = _mm_residual(hid, f2d, xs, ffn1_norm, l + 1)
                (f1g, f1u, f1d), (w_in_l,) = nxt_ffn, nxt_in
            else:
                xs = _mm_residual(hid, f2d, xs)
        outs.append(xs)
    return outs[0][None] if batch == 1 else jnp.stack(outs)
```
